```python
import math
import jax, jax.numpy as jnp
from jax import lax
import numpy as np

D_MODEL = 2048
BATCH = 1
SEQ = 8192
DEPTH = 2

N_MIXERS = 4
GROUP_W = D_MODEL // N_MIXERS
D_MIX = N_MIXERS * GROUP_W
D_FF = 5632
S5_CH = 16
S5_GROUPS = GROUP_W // S5_CH
S5_STATE = 64
SB_HEADS = 4
SB_HEAD_DIM = GROUP_W // SB_HEADS
SB_BLOCK = 128
RW_HEAD_DIM = 64
RW_HEADS = GROUP_W // RW_HEAD_DIM
RW_DECAY_LORA = 64
RW_A_LORA = 64
RW_GATE_LORA = 128
RW_GN_EPS = 64e-5
RG_BLOCKS = 8
RG_BLOCK_W = GROUP_W // RG_BLOCKS
RG_CONV = 4
RG_C = 8.0
N_IN_SLICES = 10
D_IN = N_IN_SLICES * GROUP_W
NORM_EPS = 1e-6

kernel_name = "hymba_style_s5_stickbreak_rwkv7_rglru_macaron"

F32 = jnp.float32


def rms_norm(x, g, eps=NORM_EPS):
    xf = x.astype(F32)
    y = xf * lax.rsqrt(jnp.mean(xf * xf, axis=-1, keepdims=True) + eps)
    return (y * g.astype(F32)).astype(x.dtype)


def swiglu(x, w1, w3, w2):
    return (jax.nn.silu(x @ w1) * (x @ w3)) @ w2


def token_shift(z):
    return jnp.pad(z, ((0, 0), (1, 0), (0, 0)))[:, :-1]


def real_linear_scan(a, b):
    def comb(l, r):
        return r[0] * l[0], r[0] * l[1] + r[1]
    return lax.associative_scan(comb, (a, b), axis=1)[1]


def s5_mixer(u, lam_re, lam_im, log_dt, b_re, b_im, c_re, c_im, d, glu_w, glu_b):
    bn, t, _ = u.shape
    uf = u.astype(F32).reshape(bn, t, S5_GROUPS, S5_CH)
    dt = jnp.exp(log_dt.astype(F32))[:, None]
    lr, li = lam_re.astype(F32), lam_im.astype(F32)
    mag = jnp.exp(lr * dt)
    ab_re, ab_im = mag * jnp.cos(li * dt), mag * jnp.sin(li * dt)
    den = lr * lr + li * li
    g_re = ((ab_re - 1.0) * lr + ab_im * li) / den
    g_im = (ab_im * lr - (ab_re - 1.0) * li) / den
    br, bi = b_re.astype(F32), b_im.astype(F32)
    bb_re = g_re[..., None] * br - g_im[..., None] * bi
    bb_im = g_re[..., None] * bi + g_im[..., None] * br
    bu_re = jnp.einsum('btgc,gnc->btgn', uf, bb_re)
    bu_im = jnp.einsum('btgc,gnc->btgn', uf, bb_im)
    a_re = jnp.broadcast_to(ab_re, bu_re.shape)
    a_im = jnp.broadcast_to(ab_im, bu_im.shape)

    def comb(l, r):
        lar, lai, lbr, lbi = l
        rar, rai, rbr, rbi = r
        return (rar * lar - rai * lai, rar * lai + rai * lar,
                rar * lbr - rai * lbi + rbr, rar * lbi + rai * lbr + rbi)

    _, _, s_re, s_im = lax.associative_scan(comb, (a_re, a_im, bu_re, bu_im), axis=1)
    y = (jnp.einsum('btgn,gcn->btgc', s_re, c_re.astype(F32))
         - jnp.einsum('btgn,gcn->btgc', s_im, c_im.astype(F32)))
    y = (y + d.astype(F32).reshape(S5_GROUPS, S5_CH) * uf).reshape(bn, t, GROUP_W)
    v = jax.nn.gelu(y)
    return v * jax.nn.sigmoid(v @ glu_w.astype(F32) + glu_b.astype(F32))


def stick_breaking_mixer(q, k, v, q_gain, k_gain):
    bn, t, _ = q.shape
    heads = lambda z: z.reshape(bn, t, SB_HEADS, SB_HEAD_DIM)
    qf = rms_norm(heads(q), q_gain).astype(F32) * (SB_HEAD_DIM ** -0.5)
    kf = rms_norm(heads(k), k_gain).astype(F32)
    vf = heads(v).astype(F32)
    nb = t // SB_BLOCK
    q_blocks = qf.reshape(bn, nb, SB_BLOCK, SB_HEADS, SB_HEAD_DIM).transpose(1, 0, 2, 3, 4)
    key_pos = jnp.arange(t)

    def one_block(args):
        qb, start = args
        z = jnp.einsum('bqhd,bkhd->bhqk', qb, kf)
        qpos = start + jnp.arange(SB_BLOCK)
        mask = key_pos[None, :] < qpos[:, None]
        log_beta = jax.nn.log_sigmoid(z)
        log_1m_beta = jnp.where(mask, log_beta - z, 0.0)
        after = lax.cumsum(log_1m_beta, axis=3, reverse=True) - log_1m_beta
        w = jnp.where(mask, jnp.exp(log_beta + after), 0.0)
        return jnp.einsum('bhqk,bkhd->bqhd', w, vf)

    starts = jnp.arange(nb, dtype=jnp.int32) * SB_BLOCK
    o = lax.map(one_block, (q_blocks, starts))
    return o.transpose(1, 0, 2, 3, 4).reshape(bn, t, GROUP_W)


def rwkv7_mixer(r_in, k_in, v_in, m_in, mu, w0, w_a, w_b, a0, a_a, a_b, g_a, g_b,
                k_k, k_a, r_k, ln_g, ln_b):
    bn, t, _ = r_in.shape
    r_in, k_in, v_in, m_in = (z.astype(F32) for z in (r_in, k_in, v_in, m_in))
    mu = mu.astype(F32)
    lerp = lambda z, m: z + (token_shift(z) - z) * m
    r = lerp(r_in, mu[0])
    k = lerp(k_in, mu[1])
    v = lerp(v_in, mu[2])
    xw = lerp(m_in, mu[3])
    xa = lerp(m_in, mu[4])
    xg = lerp(m_in, mu[5])
    w_log = -jax.nn.softplus(-(w0.astype(F32) + jnp.tanh(xw @ w_a.astype(F32)) @ w_b.astype(F32))) - 0.5
    decay = jnp.exp(-jnp.exp(w_log))
    a = jax.nn.sigmoid(a0.astype(F32) + (xa @ a_a.astype(F32)) @ a_b.astype(F32))
    g = jax.nn.sigmoid(xg @ g_a.astype(F32)) @ g_b.astype(F32)
    heads = lambda z: z.reshape(bn, t, RW_HEADS, RW_HEAD_DIM)
    kk = heads(k * k_k.astype(F32))
    kk = kk * lax.rsqrt(jnp.sum(kk * kk, axis=-1, keepdims=True) + 1e-12)
    k = k * (1.0 + (a - 1.0) * k_a.astype(F32))
    r_h, w_h, k_h, v_h, a_h = (heads(z) for z in (r, decay, k, v, a))

    def step(state, inp):
        rt, wt, kt, vt, at_, bt = inp
        sa = jnp.einsum('bhvk,bhk->bhv', state, at_)
        state = (state * wt[:, :, None, :] + sa[..., None] * bt[:, :, None, :]
                 + vt[..., None] * kt[:, :, None, :])
        return state, jnp.einsum('bhvk,bhk->bhv', state, rt)

    tm = lambda z: jnp.swapaxes(z, 0, 1)
    s0 = jnp.zeros((bn, RW_HEADS, RW_HEAD_DIM, RW_HEAD_DIM), F32)
    _, y = lax.scan(step, s0, (tm(r_h), tm(w_h), tm(k_h), tm(v_h), tm(-kk), tm(kk * a_h)))
    y = tm(y)
    mean = jnp.mean(y, axis=-1, keepdims=True)
    var = jnp.mean(jnp.square(y - mean), axis=-1, keepdims=True)
    y = ((y - mean) * lax.rsqrt(var + RW_GN_EPS)).reshape(bn, t, GROUP_W)
    y = y * ln_g.astype(F32) + ln_b.astype(F32)
    bonus = jnp.sum(r_h * k_h * r_k.astype(F32), axis=-1, keepdims=True) * v_h
    return (y + bonus.reshape(bn, t, GROUP_W)) * g


def rglru_mixer(gate_in, x_in, conv_w, conv_b, w_x, b_x, w_a, b_a, lam):
    bn, t, _ = x_in.shape
    xf = x_in.astype(F32)
    kern = conv_w.astype(F32)[:, None, :]
    xc = lax.conv_general_dilated(xf, kern, window_strides=(1,), padding=[(RG_CONV - 1, 0)],
                                  dimension_numbers=('NWC', 'WIO', 'NWC'),
                                  feature_group_count=GROUP_W) + conv_b.astype(F32)
    xb = xc.reshape(bn, t, RG_BLOCKS, RG_BLOCK_W)
    gate_x = jax.nn.sigmoid(jnp.einsum('btni,nij->btnj', xb, w_x.astype(F32)).reshape(bn, t, GROUP_W)
                            + b_x.astype(F32))
    gate_a = jax.nn.sigmoid(jnp.einsum('btni,nij->btnj', xb, w_a.astype(F32)).reshape(bn, t, GROUP_W)
                            + b_a.astype(F32))
    log_a = RG_C * gate_a * jax.nn.log_sigmoid(lam.astype(F32))
    a = jnp.exp(log_a)
    mult = jnp.sqrt(-jnp.expm1(2.0 * log_a))
    mult = jnp.where((jnp.arange(t) == 0)[None, :, None], 1.0, mult)
    h = real_linear_scan(a, xc * gate_x * mult)
    return jax.nn.gelu(gate_in.astype(F32)) * h


def setup_inputs(seed: int = 0) -> dict:
    key = jax.random.key(seed)
    ks = iter(jax.random.split(key, 80))
    L = DEPTH
    nrm = lambda shape, scale: jax.random.normal(next(ks), shape, F32) * scale
    gain = lambda shape: 1.0 + nrm(shape, 0.02)
    uni = lambda shape, lo, hi: jax.random.uniform(next(ks), shape, F32, minval=lo, maxval=hi)

    n_idx = jnp.arange(S5_STATE, dtype=F32)
    ratio = jnp.arange(GROUP_W, dtype=F32) / (GROUP_W - 1)
    u_a = uni((L, GROUP_W), 0.9, 0.999)
    s_a = u_a ** (1.0 / RG_C)

    return {
        "x": nrm((BATCH, SEQ, D_MODEL), 1.0),
        "ffn1_norm": gain((L, D_MODEL)),
        "ffn1_w1": nrm((L, D_MODEL, D_FF), D_MODEL ** -0.5),
        "ffn1_w3": nrm((L, D_MODEL, D_FF), D_MODEL ** -0.5),
        "ffn1_w2": nrm((L, D_FF, D_MODEL), D_FF ** -0.5),
        "mix_norm": gain((L, D_MODEL)),
        "w_in": nrm((L, D_MODEL, D_IN), D_MODEL ** -0.5),
        "s5_lambda_re": -0.5 + nrm((L, S5_GROUPS, S5_STATE), 0.01),
        "s5_lambda_im": jnp.pi * n_idx + nrm((L, S5_GROUPS, S5_STATE), 0.01),
        "s5_log_dt": uni((L, S5_GROUPS), math.log(1e-3), math.log(1e-1)),
        "s5_b_re": nrm((L, S5_GROUPS, S5_STATE, S5_CH), (2 * S5_CH) ** -0.5),
        "s5_b_im": nrm((L, S5_GROUPS, S5_STATE, S5_CH), (2 * S5_CH) ** -0.5),
        "s5_c_re": nrm((L, S5_GROUPS, S5_CH, S5_STATE), (2 * S5_STATE) ** -0.5),
        "s5_c_im": nrm((L, S5_GROUPS, S5_CH, S5_STATE), (2 * S5_STATE) ** -0.5),
        "s5_d": nrm((L, GROUP_W), 1.0),
        "s5_glu_w": nrm((L, GROUP_W, GROUP_W), GROUP_W ** -0.5),
        "s5_glu_b": nrm((L, GROUP_W), 0.02),
        "sb_q_gain": gain((L, SB_HEAD_DIM)),
        "sb_k_gain": gain((L, SB_HEAD_DIM)),
        "rw_mu": uni((L, 6, GROUP_W), 0.0, 1.0),
        "rw_w0": -7.0 + 5.0 * ratio ** 0.85 + nrm((L, GROUP_W), 0.1),
        "rw_w_a": nrm((L, GROUP_W, RW_DECAY_LORA), GROUP_W ** -0.5),
        "rw_w_b": nrm((L, RW_DECAY_LORA, GROUP_W), 0.5 * RW_DECAY_LORA ** -0.5),
        "rw_a0": nrm((L, GROUP_W), 0.1),
        "rw_a_a": nrm((L, GROUP_W, RW_A_LORA), GROUP_W ** -0.5),
        "rw_a_b": nrm((L, RW_A_LORA, GROUP_W), 0.5 * RW_A_LORA ** -0.5),
        "rw_g_a": nrm((L, GROUP_W, RW_GATE_LORA), GROUP_W ** -0.5),
        "rw_g_b": nrm((L, RW_GATE_LORA, GROUP_W), RW_GATE_LORA ** -0.5),
        "rw_k_k": 0.85 + nrm((L, GROUP_W), 0.02),
        "rw_k_a": 1.0 + nrm((L, GROUP_W), 0.02),
        "rw_r_k": nrm((L, RW_HEADS, RW_HEAD_DIM), 0.1),
        "rw_ln_g": gain((L, GROUP_W)),
        "rw_ln_b": nrm((L, GROUP_W), 0.02),
        "rg_conv_w": nrm((L, RG_CONV, GROUP_W), RG_CONV ** -0.5),
        "rg_conv_b": nrm((L, GROUP_W), 0.02),
        "rg_w_x": nrm((L, RG_BLOCKS, RG_BLOCK_W, RG_BLOCK_W), RG_BLOCK_W ** -0.5),
        "rg_b_x": nrm((L, GROUP_W), 0.02),
        "rg_w_a": nrm((L, RG_BLOCKS, RG_BLOCK_W, RG_BLOCK_W), RG_BLOCK_W ** -0.5),
        "rg_b_a": nrm((L, GROUP_W), 0.02),
        "rg_lambda": jnp.log(s_a) - jnp.log1p(-s_a),
        "out_norm": gain((L, D_MIX)),
        "w_out": nrm((L, D_MIX, D_MODEL), D_MIX ** -0.5),
        "ffn2_norm": gain((L, D_MODEL)),
        "ffn2_w1": nrm((L, D_MODEL, D_FF), D_MODEL ** -0.5),
        "ffn2_w3": nrm((L, D_MODEL, D_FF), D_MODEL ** -0.5),
        "ffn2_w2": nrm((L, D_FF, D_MODEL), D_FF ** -0.5),
    }


def reference(x, ffn1_norm, ffn1_w1, ffn1_w3, ffn1_w2, mix_norm, w_in,
              s5_lambda_re, s5_lambda_im, s5_log_dt, s5_b_re, s5_b_im, s5_c_re, s5_c_im,
              s5_d, s5_glu_w, s5_glu_b, sb_q_gain, sb_k_gain,
              rw_mu, rw_w0, rw_w_a, rw_w_b, rw_a0, rw_a_a, rw_a_b, rw_g_a, rw_g_b,
              rw_k_k, rw_k_a, rw_r_k, rw_ln_g, rw_ln_b,
              rg_conv_w, rg_conv_b, rg_w_x, rg_b_x, rg_w_a, rg_b_a, rg_lambda,
              out_norm, w_out, ffn2_norm, ffn2_w1, ffn2_w3, ffn2_w2):
    bn, t, _ = x.shape
    for l in range(DEPTH):
        h = rms_norm(x, ffn1_norm[l])
        x = x + 0.5 * swiglu(h, ffn1_w1[l], ffn1_w3[l], ffn1_w2[l])
        h = rms_norm(x, mix_norm[l])
        p = jnp.split(h @ w_in[l], N_IN_SLICES, axis=-1)
        y_a = s5_mixer(p[0], s5_lambda_re[l], s5_lambda_im[l], s5_log_dt[l], s5_b_re[l],
                       s5_b_im[l], s5_c_re[l], s5_c_im[l], s5_d[l], s5_glu_w[l], s5_glu_b[l])
        y_b = stick_breaking_mixer(p[1], p[2], p[3], sb_q_gain[l], sb_k_gain[l])
        y_c = rwkv7_mixer(p[4], p[5], p[6], p[7], rw_mu[l], rw_w0[l], rw_w_a[l], rw_w_b[l],
                          rw_a0[l], rw_a_a[l], rw_a_b[l], rw_g_a[l], rw_g_b[l],
                          rw_k_k[l], rw_k_a[l], rw_r_k[l], rw_ln_g[l], rw_ln_b[l])
        y_d = rglru_mixer(p[8], p[9], rg_conv_w[l], rg_conv_b[l], rg_w_x[l], rg_b_x[l],
                          rg_w_a[l], rg_b_a[l], rg_lambda[l])
        y = jnp.stack([y_a, y_b, y_c, y_d], axis=2).astype(x.dtype)
        y = rms_norm(y, out_norm[l].reshape(N_MIXERS, GROUP_W)).reshape(bn, t, D_MIX)
        x = x + y @ w_out[l]
        h = rms_norm(x, ffn2_norm[l])
        x = x + 0.5 * swiglu(h, ffn2_w1[l], ffn2_w3[l], ffn2_w2[l])
    return x
```

```python
import functools
import math

import jax
import jax.numpy as jnp
from jax import lax
from jax.experimental import pallas as pl
from jax.experimental.pallas import tpu as pltpu

F32 = jnp.float32
BF16 = jnp.bfloat16

D_MODEL = 2048
GROUP_W = 512
D_FF = 5632
D_IN = 10 * GROUP_W
S5_CH = 16
S5_GROUPS = 32
S5_STATE = 64
S5_LANES = S5_GROUPS * S5_STATE
SB_HEADS = 4
SB_HEAD_DIM = 128
RW_HEAD_DIM = 64
RW_GN_EPS = 64e-5
RG_CONV = 4
RG_C = 8.0
NORM_EPS = 1e-6

SUBLANES = 8
VMEM_LIMIT = 56 * 1024 * 1024

HI = lax.Precision.HIGHEST


def _cparams(*sem):
    return pltpu.CompilerParams(dimension_semantics=sem, vmem_limit_bytes=VMEM_LIMIT)


def _dot(a, b):
    return jnp.dot(a, b, preferred_element_type=F32)


def _dot_nt(a, b):
    return lax.dot_general(a, b, (((1,), (1,)), ((), ())), preferred_element_type=F32)


def _split_dot(x, m_bf16):
    hi = x.astype(BF16)
    lo = (x - hi.astype(F32)).astype(BF16)
    return _dot(hi, m_bf16) + _dot(lo, m_bf16)


def _rms(x, g):
    return x * lax.rsqrt(jnp.mean(x * x, axis=-1, keepdims=True) + NORM_EPS) * g


def _sigmoid(x):
    return 1.0 / (1.0 + jnp.exp(-x))


def _softplus(x):
    return jnp.maximum(x, 0.0) + jnp.log1p(jnp.exp(-jnp.abs(x)))


def _gelu_tanh(x):
    c = math.sqrt(2.0 / math.pi)
    return 0.5 * x * (1.0 + jnp.tanh(c * (x + 0.044715 * (x * x * x))))


def _ffn_kernel(x_ref, g_ref, w1_ref, w3_ref, w2_ref, o_ref, h_ref):
    @pl.when(pl.program_id(1) == 0)
    def _():
        x = x_ref[...]
        h_ref[...] = _rms(x, g_ref[...]).astype(BF16)
        o_ref[...] = x

    h = h_ref[...]
    a = _dot(h, w1_ref[...])
    b = _dot(h, w3_ref[...])
    act = (0.5 * (a * _sigmoid(a)) * b).astype(BF16)
    o_ref[...] += _dot(act, w2_ref[...])


def _ffn(x, g, w1, w3, w2, tm=512, tf=512):
    t = x.shape[0]
    return pl.pallas_call(
        _ffn_kernel,
        grid=(t // tm, D_FF // tf),
        in_specs=[
            pl.BlockSpec((tm, D_MODEL), lambda i, j: (i, 0)),
            pl.BlockSpec((1, D_MODEL), lambda i, j: (0, 0)),
            pl.BlockSpec((D_MODEL, tf), lambda i, j: (0, j)),
            pl.BlockSpec((D_MODEL, tf), lambda i, j: (0, j)),
            pl.BlockSpec((tf, D_MODEL), lambda i, j: (j, 0)),
        ],
        out_specs=pl.BlockSpec((tm, D_MODEL), lambda i, j: (i, 0)),
        out_shape=jax.ShapeDtypeStruct((t, D_MODEL), F32),
        scratch_shapes=[pltpu.VMEM((tm, D_MODEL), BF16)],
        compiler_params=_cparams("parallel", "arbitrary"),
        name="ffn",
    )(x, g.reshape(1, D_MODEL), w1, w3, w2)


def _inproj_kernel(x_ref, g_ref, w_ref, o_ref, h_ref):
    @pl.when(pl.program_id(1) == 0)
    def _():
        h_ref[...] = _rms(x_ref[...], g_ref[...]).astype(BF16)

    o_ref[...] = _dot(h_ref[...], w_ref[...])


def _inproj(x, g, w, tm=512, tn=1024):
    t = x.shape[0]
    return pl.pallas_call(
        _inproj_kernel,
        grid=(t // tm, D_IN // tn),
        in_specs=[
            pl.BlockSpec((tm, D_MODEL), lambda i, j: (i, 0)),
            pl.BlockSpec((1, D_MODEL), lambda i, j: (0, 0)),
            pl.BlockSpec((D_MODEL, tn), lambda i, j: (0, j)),
        ],
        out_specs=pl.BlockSpec((tm, tn), lambda i, j: (i, j)),
        out_shape=jax.ShapeDtypeStruct((t, D_IN), F32),
        scratch_shapes=[pltpu.VMEM((tm, D_MODEL), BF16)],
        compiler_params=_cparams("parallel", "arbitrary"),
        name="inproj",
    )(x, g.reshape(1, D_MODEL), w)


def _outproj_kernel(x_ref, ya_ref, yb_ref, yc_ref, yd_ref, g_ref, w_ref, o_ref):
    acc = x_ref[...]
    for n, y_ref in enumerate((ya_ref, yb_ref, yc_ref, yd_ref)):
        yn = _rms(y_ref[...], g_ref[n:n + 1, :]).astype(BF16)
        acc = acc + _dot(yn, w_ref[n * GROUP_W:(n + 1) * GROUP_W, :])
    o_ref[...] = acc


def _outproj(x, ys, g, w, tm=512):
    t = x.shape[0]
    yspec = pl.BlockSpec((tm, GROUP_W), lambda i: (i, 0))
    return pl.pallas_call(
        _outproj_kernel,
        grid=(t // tm,),
        in_specs=[
            pl.BlockSpec((tm, D_MODEL), lambda i: (i, 0)),
            yspec, yspec, yspec, yspec,
            pl.BlockSpec((4, GROUP_W), lambda i: (0, 0)),
            pl.BlockSpec((D_MODEL, D_MODEL), lambda i: (0, 0)),
        ],
        out_specs=pl.BlockSpec((tm, D_MODEL), lambda i: (i, 0)),
        out_shape=jax.ShapeDtypeStruct((t, D_MODEL), F32),
        compiler_params=_cparams("parallel"),
        name="outproj",
    )(x, *ys, g.reshape(4, GROUP_W), w)


def _s5_prep_kernel(lr_ref, li_ref, ldt_ref, br_ref, bi_ref, ab_ref, bb_ref):
    lr, li = lr_ref[...], li_ref[...]
    dt = jnp.exp(ldt_ref[...])
    mag = jnp.exp(lr * dt)
    ab_re, ab_im = mag * jnp.cos(li * dt), mag * jnp.sin(li * dt)
    den = lr * lr + li * li
    g_re = ((ab_re - 1.0) * lr + ab_im * li) / den
    g_im = (ab_im * lr - (ab_re - 1.0) * li) / den
    br, bi = br_ref[...], bi_ref[...]
    ab_ref[0:1, :] = ab_re
    ab_ref[1:2, :] = ab_im
    bb_ref[0] = g_re * br - g_im * bi
    bb_ref[1] = g_re * bi + g_im * br


def _s5_prep(lam_re, lam_im, log_dt, b_re, b_im):
    row = lambda z: z.reshape(1, S5_LANES)
    ldt = jnp.broadcast_to(log_dt[:, None], (S5_GROUPS, S5_STATE))
    tr = lambda b: b.transpose(2, 0, 1).reshape(S5_CH, S5_LANES)
    return pl.pallas_call(
        _s5_prep_kernel,
        out_shape=(jax.ShapeDtypeStruct((2, S5_LANES), F32),
                   jax.ShapeDtypeStruct((2, S5_CH, S5_LANES), F32)),
        name="s5_prep",
    )(row(lam_re), row(lam_im), row(ldt), tr(b_re), tr(b_im))


def _cmul(ar, ai, br, bi):
    return ar * br - ai * bi, ar * bi + ai * br


def _s5_kernel(u_ref, ab_ref, bmat_ref, cmat_ref, d_ref, gw_ref, gb_ref, o_ref,
               s_ref, tab_ref, car_ref, *, tc, cw):
    n = S5_LANES

    @pl.when(pl.program_id(0) == 0)
    def _():
        ar = jnp.broadcast_to(ab_ref[0:1, :], (SUBLANES, n))
        ai = jnp.broadcast_to(ab_ref[1:2, :], (SUBLANES, n))
        row = lax.broadcasted_iota(jnp.int32, (SUBLANES, n), 0)
        a2r, a2i = _cmul(ar, ai, ar, ai)
        a4r, a4i = _cmul(a2r, a2i, a2r, a2i)
        for k, (pr, pi, sh) in enumerate(((ar, ai, 1), (a2r, a2i, 2), (a4r, a4i, 4))):
            tab_ref[2 * k] = jnp.where(row >= sh, pr, 0.0)
            tab_ref[2 * k + 1] = jnp.where(row >= sh, pi, 0.0)
        pwr, pwi = ar, ai
        for bit, (qr, qi) in ((1, (ar, ai)), (2, (a2r, a2i)), (4, (a4r, a4i))):
            nr, ni = _cmul(pwr, pwi, qr, qi)
            sel = (row & bit) != 0
            pwr, pwi = jnp.where(sel, nr, pwr), jnp.where(sel, ni, pwi)
        tab_ref[6] = pwr
        tab_ref[7] = pwi
        car_ref[...] = jnp.zeros_like(car_ref)

    u = u_ref[...]
    s_ref[...] = _dot(u.astype(BF16), bmat_ref[...])

    for c in range(n // cw):
        cre = pl.ds(c * cw, cw)
        cim = pl.ds(n + c * cw, cw)
        t1r, t1i, t2r, t2i, t4r, t4i, pwr, pwi = (tab_ref[k, :, c * cw:(c + 1) * cw] for k in range(8))

        def body(i, carry):
            cr, ci = carry
            rows = pl.ds(pl.multiple_of(i * SUBLANES, SUBLANES), SUBLANES)
            xr, xi = s_ref[rows, cre], s_ref[rows, cim]
            for sh, tr_, ti_ in ((1, t1r, t1i), (2, t2r, t2i), (4, t4r, t4i)):
                dr, di = _cmul(tr_, ti_, pltpu.roll(xr, sh, 0), pltpu.roll(xi, sh, 0))
                xr, xi = xr + dr, xi + di
            dr, di = _cmul(pwr, pwi, jnp.broadcast_to(cr, xr.shape), jnp.broadcast_to(ci, xi.shape))
            xr, xi = xr + dr, xi + di
            s_ref[rows, cre] = xr
            s_ref[rows, cim] = xi
            return xr[SUBLANES - 1:SUBLANES, :], xi[SUBLANES - 1:SUBLANES, :]

        cr, ci = lax.fori_loop(0, tc // SUBLANES, body,
                               (car_ref[0:1, c * cw:(c + 1) * cw], car_ref[1:2, c * cw:(c + 1) * cw]),
                               unroll=2)
        car_ref[0:1, c * cw:(c + 1) * cw] = cr
        car_ref[1:2, c * cw:(c + 1) * cw] = ci

    y = _dot(s_ref[...].astype(BF16), cmat_ref[...]) + d_ref[...] * u
    v = _gelu_tanh(y)
    o_ref[...] = v * _sigmoid(_dot(v.astype(BF16), gw_ref[...]) + gb_ref[...])


def _s5(p, ab, bb, c_re, c_im, d, glu_w, glu_b, tc=512, cw=256):
    t = p.shape[0]
    n = S5_LANES
    grp_lane = jnp.arange(n) // S5_STATE
    grp_ch = jnp.arange(GROUP_W) // S5_CH
    same = grp_ch[:, None] == grp_lane[None, :]
    tile_b = lambda z: jnp.where(same, jnp.tile(z, (S5_GROUPS, 1)), 0.0)
    bmat = jnp.concatenate([tile_b(bb[0]), tile_b(bb[1])], axis=1).astype(BF16)
    tile_c = lambda z: jnp.where(same.T, jnp.tile(z.transpose(0, 2, 1).reshape(n, S5_CH), (1, S5_GROUPS)), 0.0)
    cmat = jnp.concatenate([tile_c(c_re), -tile_c(c_im)], axis=0).astype(BF16)
    const = lambda shape: pl.BlockSpec(shape, lambda i: (0,) * len(shape))
    return pl.pallas_call(
        functools.partial(_s5_kernel, tc=tc, cw=cw),
        grid=(t // tc,),
        in_specs=[
            pl.BlockSpec((tc, GROUP_W), lambda i: (i, 0)),
            const((2, n)), const((GROUP_W, 2 * n)), const((2 * n, GROUP_W)),
            const((1, GROUP_W)), const((GROUP_W, GROUP_W)), const((1, GROUP_W)),
        ],
        out_specs=pl.BlockSpec((tc, GROUP_W), lambda i: (i, 0)),
        out_shape=jax.ShapeDtypeStruct((t, GROUP_W), F32),
        scratch_shapes=[pltpu.VMEM((tc, 2 * n), F32),
                        pltpu.VMEM((8, SUBLANES, n), F32),
                        pltpu.VMEM((2, n), F32)],
        compiler_params=_cparams("arbitrary"),
        name="s5",
    )(p, ab, bmat, cmat, d.reshape(1, GROUP_W), glu_w.astype(BF16), glu_b.reshape(1, GROUP_W))


def _sb_kernel(q_ref, k_ref, v_ref, qg_ref, kg_ref, tri_ref, o_ref, kn_ref, vb_ref, *, tq):
    qi = pl.program_id(1)

    @pl.when(qi == 0)
    def _():
        kn_ref[...] = _rms(k_ref[...], kg_ref[...]).astype(BF16)
        vb_ref[...] = v_ref[...].astype(BF16)

    qn = (_rms(q_ref[...], qg_ref[...]) * (SB_HEAD_DIM ** -0.5)).astype(BF16)
    tri = tri_ref[...]

    def tile(j, carry, acc, diagonal):
        rows = pl.ds(pl.multiple_of(j * tq, tq), tq)
        z = _dot_nt(qn, kn_ref[rows, :])
        lb = jnp.minimum(z, 0.0) - jnp.log1p(jnp.exp(-jnp.abs(z)))
        l1m = lb - z
        if diagonal:
            keep = (lax.broadcasted_iota(jnp.int32, (tq, tq), 1)
                    < lax.broadcasted_iota(jnp.int32, (tq, tq), 0))
            l1m = jnp.where(keep, l1m, 0.0)
        after = _split_dot(l1m, tri)
        w = jnp.exp(lb + after + carry)
        if diagonal:
            w = jnp.where(keep, w, 0.0)
        acc = acc + _dot(w.astype(BF16), vb_ref[rows, :])
        return carry + jnp.sum(l1m, axis=-1, keepdims=True), acc

    carry, acc = tile(qi, jnp.zeros((tq, 1), F32), jnp.zeros((tq, SB_HEAD_DIM), F32), True)

    def body(it, ca):
        return tile(qi - 1 - it, ca[0], ca[1], False)

    _, acc = lax.fori_loop(0, qi, body, (carry, acc))
    o_ref[...] = acc


def _sb(p, q_gain, k_gain, tq=256):
    t = p.shape[0]
    tri = jnp.tril(jnp.ones((tq, tq), F32), -1).astype(BF16)
    return pl.pallas_call(
        functools.partial(_sb_kernel, tq=tq),
        grid=(SB_HEADS, t // tq),
        in_specs=[
            pl.BlockSpec((tq, SB_HEAD_DIM), lambda h, i: (i, 4 + h)),
            pl.BlockSpec((t, SB_HEAD_DIM), lambda h, i: (0, 8 + h)),
            pl.BlockSpec((t, SB_HEAD_DIM), lambda h, i: (0, 12 + h)),
            pl.BlockSpec((1, SB_HEAD_DIM), lambda h, i: (0, 0)),
            pl.BlockSpec((1, SB_HEAD_DIM), lambda h, i: (0, 0)),
            pl.BlockSpec((tq, tq), lambda h, i: (0, 0)),
        ],
        out_specs=pl.BlockSpec((tq, SB_HEAD_DIM), lambda h, i: (i, h)),
        out_shape=jax.ShapeDtypeStruct((t, GROUP_W), F32),
        scratch_shapes=[pltpu.VMEM((t, SB_HEAD_DIM), BF16), pltpu.VMEM((t, SB_HEAD_DIM), BF16)],
        compiler_params=_cparams("arbitrary", "arbitrary"),
        name="stickbreak",
    )(p, p, p, q_gain.reshape(1, SB_HEAD_DIM), k_gain.reshape(1, SB_HEAD_DIM), tri)


RW_CHUNK = 64
(_MU_R, _MU_K, _MU_V, _MU_W, _MU_A, _MU_G, _W0, _A0, _KK, _KA, _RK, _LNG, _LNB) = range(13)


def _dot_hi(a, b):
    return jnp.dot(a, b, preferred_element_type=F32, precision=HI)


def _rwkv_kernel(r_ref, k_ref, v_ref, m_ref, vec_ref, wa_ref, wb_ref, aa_ref, ab_ref, ga_ref, gb_ref,
                 ones_ref, tril_ref, o_ref,
                 zs_ref, st_ref, r_s, lw_s, k_s, v_s, a_s, b_s, y_s, *, tc):
    first = pl.program_id(0) == 0
    w = GROUP_W
    vec = lambda i: vec_ref[i:i + 1, :]

    @pl.when(first)
    def _():
        zs_ref[...] = jnp.zeros_like(zs_ref)
        st_ref[...] = jnp.zeros_like(st_ref)

    cur, prev = [], []
    for n, ref in enumerate((r_ref, k_ref, v_ref, m_ref)):
        zs_ref[n, SUBLANES - 1:SUBLANES, :] = zs_ref[n, SUBLANES + tc - 1:SUBLANES + tc, :]
        z = ref[...]
        zs_ref[n, SUBLANES:SUBLANES + tc, :] = z
        cur.append(z)
        prev.append(zs_ref[n, SUBLANES - 1:SUBLANES - 1 + tc, :])
    lerp = lambda n, mu: cur[n] + (prev[n] - cur[n]) * vec(mu)
    r, k, v = lerp(0, _MU_R), lerp(1, _MU_K), lerp(2, _MU_V)
    xw, xa, xg = lerp(3, _MU_W), lerp(3, _MU_A), lerp(3, _MU_G)

    lora = lambda x, a_ref, b_ref, f: _dot(f(_dot(x.astype(BF16), a_ref[...])).astype(BF16), b_ref[...])
    w_log = -_softplus(-(vec(_W0) + lora(xw, wa_ref, wb_ref, jnp.tanh))) - 0.5
    lw = -jnp.exp(w_log)
    a = _sigmoid(vec(_A0) + lora(xa, aa_ref, ab_ref, lambda z: z))
    g = lora(xg, ga_ref, gb_ref, _sigmoid)
    ones = ones_ref[...]
    kk = k * vec(_KK)
    kk = kk * lax.rsqrt(_split_dot(kk * kk, ones) + 1e-12)
    k = k * (1.0 + (a - 1.0) * vec(_KA))
    bonus = _split_dot(r * k * vec(_RK), ones) * v

    r_s[...] = r
    lw_s[...] = lw
    k_s[...] = k
    v_s[...] = v
    a_s[...] = -kk
    b_s[...] = kk * a

    L = RW_CHUNK
    lane = lax.broadcasted_iota(jnp.int32, (L, 128), 1)
    head0 = lane < RW_HEAD_DIM
    ri = lax.broadcasted_iota(jnp.int32, (2 * L, 2 * L), 0)
    ci = lax.broadcasted_iota(jnp.int32, (2 * L, 2 * L), 1)
    strict, incl = ri > ci, ri >= ci
    eye = (ri == ci).astype(F32)

    def stack(x):
        return jnp.concatenate([jnp.where(head0, x, 0.0), jnp.where(head0, 0.0, x)], axis=0)

    def chunk(c, _):
        rows = pl.ds(pl.multiple_of(c * L, L), L)
        lwc = lw_s[rows, :]
        cum = _split_dot_left(tril_ref[...], lwc)
        cum_l = cum[L - 1:L, :]
        g_in, g_ex, g_inv, g_end = jnp.exp(cum), jnp.exp(cum - lwc), jnp.exp(-cum), jnp.exp(cum_l - cum)
        rt, at = r_s[rows, :] * g_in, a_s[rows, :] * g_ex
        bc, kc, vc = b_s[rows, :], k_s[rows, :], v_s[rows, :]
        bt, kt, bl, kl = bc * g_inv, kc * g_inv, bc * g_end, kc * g_end
        ys = []
        for hp in range(w // 128):
            sl = slice(hp * 128, (hp + 1) * 128)
            xa_, xr_, xb_, xk_, xbl, xkl, xv_ = (stack(z[:, sl]) for z in (at, rt, bt, kt, bl, kl, vc))
            st = st_ref[hp]
            lhs = jnp.concatenate([xa_, xr_], axis=0)
            pm = lax.dot_general(lhs, jnp.concatenate([xb_, xk_], axis=0), (((1,), (1,)), ((), ())),
                                 preferred_element_type=F32, precision=HI)
            n_ab = jnp.where(strict, pm[:2 * L, :2 * L], 0.0)
            a_ak = jnp.where(strict, pm[:2 * L, 2 * L:], 0.0)
            a_rb = jnp.where(incl, pm[2 * L:, :2 * L], 0.0)
            a_rk = jnp.where(incl, pm[2 * L:, 2 * L:], 0.0)
            gs = _dot_hi(lhs, st)
            tinv, npow = eye + n_ab, n_ab
            for _ in range(int(math.log2(L)) - 1):
                npow = _dot_hi(npow, npow)
                tinv = tinv + _dot_hi(npow, tinv)
            u = _dot_hi(tinv, gs[:2 * L] + _dot_hi(a_ak, xv_))
            uv = jnp.concatenate([u, xv_], axis=0)
            yst = gs[2 * L:] + _dot_hi(jnp.concatenate([a_rb, a_rk], axis=1), uv)
            ys.append(yst[:L] + yst[L:])
            g_col = jnp.transpose(jnp.broadcast_to(jnp.exp(cum_l[:, sl]), (128, 128)))
            st_ref[hp] = g_col * st + lax.dot_general(
                jnp.concatenate([xbl, xkl], axis=0), uv, (((0,), (0,)), ((), ())),
                preferred_element_type=F32, precision=HI)
        y_s[rows, :] = jnp.concatenate(ys, axis=1)
        return 0

    lax.fori_loop(0, tc // L, chunk, 0)

    y = y_s[...]
    inv_n = 1.0 / RW_HEAD_DIM
    mean = _split_dot(y, ones) * inv_n
    yc = y - mean
    var = _split_dot(yc * yc, ones) * inv_n
    y = yc * lax.rsqrt(var + RW_GN_EPS) * vec(_LNG) + vec(_LNB)
    o_ref[...] = (y + bonus) * g


def _split_dot_left(m_bf16, x):
    hi = x.astype(BF16)
    lo = (x - hi.astype(F32)).astype(BF16)
    return _dot(m_bf16, hi) + _dot(m_bf16, lo)


def _rwkv(p, mu, w0, w_a, w_b, a0, a_a, a_b, g_a, g_b, k_k, k_a, r_k, ln_g, ln_b, tc=256):
    t = p.shape[0]
    w = GROUP_W
    vecs = jnp.concatenate([mu, jnp.stack([w0, a0, k_k, k_a, r_k.reshape(w), ln_g, ln_b]),
                            jnp.zeros((3, w), F32)], axis=0)
    head = jnp.arange(w) // RW_HEAD_DIM
    ones = (head[:, None] == head[None, :]).astype(BF16)
    tril = jnp.tril(jnp.ones((RW_CHUNK, RW_CHUNK), F32)).astype(BF16)
    const = lambda shape: pl.BlockSpec(shape, lambda i: (0,) * len(shape))
    col = lambda c: pl.BlockSpec((tc, w), lambda i: (i, c))
    bf = lambda z: z.astype(BF16)
    big = lambda: pltpu.VMEM((tc, w), F32)
    return pl.pallas_call(
        functools.partial(_rwkv_kernel, tc=tc),
        grid=(t // tc,),
        in_specs=[col(4), col(5), col(6), col(7), const((16, w)),
                  const(w_a.shape), const(w_b.shape), const(a_a.shape), const(a_b.shape),
                  const(g_a.shape), const(g_b.shape), const((w, w)), const((RW_CHUNK, RW_CHUNK))],
        out_specs=pl.BlockSpec((tc, w), lambda i: (i, 0)),
        out_shape=jax.ShapeDtypeStruct((t, w), F32),
        scratch_shapes=[pltpu.VMEM((4, tc + SUBLANES, w), F32), pltpu.VMEM((w // 128, 128, 128), F32),
                        big(), big(), big(), big(), big(), big(), big()],
        compiler_params=_cparams("arbitrary"),
        name="rwkv7",
    )(p, p, p, p, vecs, bf(w_a), bf(w_b), bf(a_a), bf(a_b), bf(g_a), bf(g_b), ones, tril)


def _rglru_kernel(gate_ref, x_ref, cw_ref, vec_ref, wx_ref, wa_ref, o_ref,
                  xs_ref, a_s, b_s, car_ref, *, tc):
    first = pl.program_id(0) == 0
    halo = SUBLANES

    @pl.when(first)
    def _():
        xs_ref[...] = jnp.zeros_like(xs_ref)
        car_ref[...] = jnp.zeros_like(car_ref)

    xs_ref[0:halo, :] = xs_ref[tc:tc + halo, :]
    xs_ref[halo:halo + tc, :] = x_ref[...]
    xc = vec_ref[0:1, :]
    for j in range(RG_CONV):
        off = halo - (RG_CONV - 1) + j
        xc = xc + xs_ref[off:off + tc, :] * cw_ref[j:j + 1, :]
    xcb = xc.astype(BF16)
    gate_x = _sigmoid(_dot(xcb, wx_ref[...]) + vec_ref[1:2, :])
    gate_a = _sigmoid(_dot(xcb, wa_ref[...]) + vec_ref[2:3, :])
    lam = vec_ref[3:4, :]
    log_a = RG_C * gate_a * (-_softplus(-lam))
    a = jnp.exp(log_a)
    mult = jnp.sqrt(1.0 - a * a)
    t_idx = lax.broadcasted_iota(jnp.int32, (tc, GROUP_W), 0)
    mult = jnp.where(jnp.logical_and(first, t_idx == 0), 1.0, mult)
    a_s[...] = a
    b_s[...] = xc * gate_x * mult

    row = lax.broadcasted_iota(jnp.int32, (SUBLANES, GROUP_W), 0)

    def body(i, carry):
        rows = pl.ds(pl.multiple_of(i * SUBLANES, SUBLANES), SUBLANES)
        ab, hb = a_s[rows, :], b_s[rows, :]
        for sh in (1, 2, 4):
            ash = jnp.where(row >= sh, pltpu.roll(ab, sh, 0), 1.0)
            hsh = jnp.where(row >= sh, pltpu.roll(hb, sh, 0), 0.0)
            hb = hb + ab * hsh
            ab = ab * ash
        hb = hb + ab * jnp.broadcast_to(carry, hb.shape)
        b_s[rows, :] = hb
        return hb[SUBLANES - 1:SUBLANES, :]

    car_ref[...] = lax.fori_loop(0, tc // SUBLANES, body, car_ref[...], unroll=2)
    o_ref[...] = _gelu_tanh(gate_ref[...]) * b_s[...]


def _rglru(p, conv_w, conv_b, w_x, b_x, w_a, b_a, lam, tc=512):
    t = p.shape[0]
    w = GROUP_W
    blk = jnp.arange(w) // (w // w_x.shape[0])
    same = blk[:, None] == blk[None, :]
    bdiag = lambda m: jnp.where(same, jnp.tile(m.reshape(w, -1), (1, m.shape[0])), 0.0).astype(BF16)
    vecs = jnp.concatenate([jnp.stack([conv_b, b_x, b_a, lam]), jnp.zeros((4, w), F32)], axis=0)
    const = lambda shape: pl.BlockSpec(shape, lambda i: (0,) * len(shape))
    return pl.pallas_call(
        functools.partial(_rglru_kernel, tc=tc),
        grid=(t // tc,),
        in_specs=[pl.BlockSpec((tc, w), lambda i: (i, 8)), pl.BlockSpec((tc, w), lambda i: (i, 9)),
                  const((RG_CONV, w)), const((8, w)), const((w, w)), const((w, w))],
        out_specs=pl.BlockSpec((tc, w), lambda i: (i, 0)),
        out_shape=jax.ShapeDtypeStruct((t, w), F32),
        scratch_shapes=[pltpu.VMEM((tc + SUBLANES, w), F32), pltpu.VMEM((tc, w), F32),
                        pltpu.VMEM((tc, w), F32), pltpu.VMEM((1, w), F32)],
        compiler_params=_cparams("arbitrary"),
        name="rglru",
    )(p, p, conv_w, vecs, bdiag(w_x), bdiag(w_a))


def kernel(x, ffn1_norm, ffn1_w1, ffn1_w3, ffn1_w2, mix_norm, w_in, s5_lambda_re, s5_lambda_im, s5_log_dt, s5_b_re, s5_b_im, s5_c_re, s5_c_im, s5_d, s5_glu_w, s5_glu_b, sb_q_gain, sb_k_gain, rw_mu, rw_w0, rw_w_a, rw_w_b, rw_a0, rw_a_a, rw_a_b, rw_g_a, rw_g_b, rw_k_k, rw_k_a, rw_r_k, rw_ln_g, rw_ln_b, rg_conv_w, rg_conv_b, rg_w_x, rg_b_x, rg_w_a, rg_b_a, rg_lambda, out_norm, w_out, ffn2_norm, ffn2_w1, ffn2_w3, ffn2_w2):
    bn, t, _ = x.shape
    assert bn == 1
    x = x.reshape(t, D_MODEL)
    bf = lambda z: z.astype(BF16)
    for l in range(ffn1_norm.shape[0]):
        x = _ffn(x, ffn1_norm[l], bf(ffn1_w1[l]), bf(ffn1_w3[l]), bf(ffn1_w2[l]))
        p = _inproj(x, mix_norm[l], bf(w_in[l]))
        ab, bb = _s5_prep(s5_lambda_re[l], s5_lambda_im[l], s5_log_dt[l], s5_b_re[l], s5_b_im[l])
        y_a = _s5(p, ab, bb, s5_c_re[l], s5_c_im[l], s5_d[l], s5_glu_w[l], s5_glu_b[l])
        y_b = _sb(p, sb_q_gain[l], sb_k_gain[l])
        y_c = _rwkv(p, rw_mu[l], rw_w0[l], rw_w_a[l], rw_w_b[l], rw_a0[l], rw_a_a[l], rw_a_b[l],
                    rw_g_a[l], rw_g_b[l], rw_k_k[l], rw_k_a[l], rw_r_k[l], rw_ln_g[l], rw_ln_b[l])
        y_d = _rglru(p, rg_conv_w[l], rg_conv_b[l], rg_w_x[l], rg_b_x[l], rg_w_a[l], rg_b_a[l],
                     rg_lambda[l])
        x = _outproj(x, (y_a, y_b, y_c, y_d), out_norm[l], bf(w_out[l]))
        x = _ffn(x, ffn2_norm[l], bf(ffn2_w1[l]), bf(ffn2_w3[l]), bf(ffn2_w2[l]))
    return x.reshape(bn, t, D_MODEL)
```

```python
import functools
import math

import jax
import jax.numpy as jnp
from jax import lax
from jax.experimental import pallas as pl
from jax.experimental.pallas import tpu as pltpu

F32 = jnp.float32
BF16 = jnp.bfloat16

D_MODEL = 2048
GROUP_W = 512
D_FF = 5632
D_IN = 10 * GROUP_W
S5_CH = 16
S5_GROUPS = 32
S5_STATE = 64
S5_LANES = S5_GROUPS * S5_STATE
SB_HEADS = 4
SB_HEAD_DIM = 128
RW_HEAD_DIM = 64
RW_GN_EPS = 64e-5
RG_CONV = 4
RG_C = 8.0
NORM_EPS = 1e-6

SUBLANES = 8
VMEM_LIMIT = 56 * 1024 * 1024

HI = lax.Precision.HIGHEST


def _cparams(*sem):
    return pltpu.CompilerParams(dimension_semantics=sem, vmem_limit_bytes=VMEM_LIMIT)


def _dot(a, b):
    return jnp.dot(a, b, preferred_element_type=F32)


def _dot_nt(a, b):
    return lax.dot_general(a, b, (((1,), (1,)), ((), ())), preferred_element_type=F32)


def _split_dot(x, m_bf16):
    hi = x.astype(BF16)
    lo = (x - hi.astype(F32)).astype(BF16)
    return _dot(hi, m_bf16) + _dot(lo, m_bf16)


def _rms(x, g):
    return x * lax.rsqrt(jnp.mean(x * x, axis=-1, keepdims=True) + NORM_EPS) * g


def _sigmoid(x):
    return 1.0 / (1.0 + jnp.exp(-x))


def _softplus(x):
    return jnp.maximum(x, 0.0) + jnp.log1p(jnp.exp(-jnp.abs(x)))


def _gelu_tanh(x):
    c = math.sqrt(2.0 / math.pi)
    return 0.5 * x * (1.0 + jnp.tanh(c * (x + 0.044715 * (x * x * x))))


def _ffn_kernel(x_ref, g_ref, w1_ref, w3_ref, w2_ref, o_ref, h_ref):
    @pl.when(pl.program_id(1) == 0)
    def _():
        x = x_ref[...]
        h_ref[...] = _rms(x, g_ref[...]).astype(BF16)
        o_ref[...] = x

    h = h_ref[...]
    a = _dot(h, w1_ref[...])
    b = _dot(h, w3_ref[...])
    act = (0.5 * (a * _sigmoid(a)) * b).astype(BF16)
    o_ref[...] += _dot(act, w2_ref[...])


def _ffn(x, g, w1, w3, w2, tm=512, tf=512):
    t = x.shape[0]
    return pl.pallas_call(
        _ffn_kernel,
        grid=(t // tm, D_FF // tf),
        in_specs=[
            pl.BlockSpec((tm, D_MODEL), lambda i, j: (i, 0)),
            pl.BlockSpec((1, D_MODEL), lambda i, j: (0, 0)),
            pl.BlockSpec((D_MODEL, tf), lambda i, j: (0, j)),
            pl.BlockSpec((D_MODEL, tf), lambda i, j: (0, j)),
            pl.BlockSpec((tf, D_MODEL), lambda i, j: (j, 0)),
        ],
        out_specs=pl.BlockSpec((tm, D_MODEL), lambda i, j: (i, 0)),
        out_shape=jax.ShapeDtypeStruct((t, D_MODEL), F32),
        scratch_shapes=[pltpu.VMEM((tm, D_MODEL), BF16)],
        compiler_params=_cparams("parallel", "arbitrary"),
        name="ffn",
    )(x, g.reshape(1, D_MODEL), w1, w3, w2)


def _inproj_kernel(x_ref, g_ref, w_ref, o_ref, h_ref):
    @pl.when(pl.program_id(1) == 0)
    def _():
        h_ref[...] = _rms(x_ref[...], g_ref[...]).astype(BF16)

    o_ref[...] = _dot(h_ref[...], w_ref[...])


def _inproj(x, g, w, tm=512, tn=1024):
    t = x.shape[0]
    return pl.pallas_call(
        _inproj_kernel,
        grid=(t // tm, D_IN // tn),
        in_specs=[
            pl.BlockSpec((tm, D_MODEL), lambda i, j: (i, 0)),
            pl.BlockSpec((1, D_MODEL), lambda i, j: (0, 0)),
            pl.BlockSpec((D_MODEL, tn), lambda i, j: (0, j)),
        ],
        out_specs=pl.BlockSpec((tm, tn), lambda i, j: (i, j)),
        out_shape=jax.ShapeDtypeStruct((t, D_IN), F32),
        scratch_shapes=[pltpu.VMEM((tm, D_MODEL), BF16)],
        compiler_params=_cparams("parallel", "arbitrary"),
        name="inproj",
    )(x, g.reshape(1, D_MODEL), w)


def _outproj_kernel(x_ref, ya_ref, yb_ref, yc_ref, yd_ref, g_ref, w_ref, o_ref):
    acc = x_ref[...]
    for n, y_ref in enumerate((ya_ref, yb_ref, yc_ref, yd_ref)):
        yn = _rms(y_ref[...], g_ref[n:n + 1, :]).astype(BF16)
        acc = acc + _dot(yn, w_ref[n * GROUP_W:(n + 1) * GROUP_W, :])
    o_ref[...] = acc


def _outproj(x, ys, g, w, tm=512):
    t = x.shape[0]
    yspec = pl.BlockSpec((tm, GROUP_W), lambda i: (i, 0))
    return pl.pallas_call(
        _outproj_kernel,
        grid=(t // tm,),
        in_specs=[
            pl.BlockSpec((tm, D_MODEL), lambda i: (i, 0)),
            yspec, yspec, yspec, yspec,
            pl.BlockSpec((4, GROUP_W), lambda i: (0, 0)),
            pl.BlockSpec((D_MODEL, D_MODEL), lambda i: (0, 0)),
        ],
        out_specs=pl.BlockSpec((tm, D_MODEL), lambda i: (i, 0)),
        out_shape=jax.ShapeDtypeStruct((t, D_MODEL), F32),
        compiler_params=_cparams("parallel"),
        name="outproj",
    )(x, *ys, g.reshape(4, GROUP_W), w)


def _s5_prep_kernel(lr_ref, li_ref, ldt_ref, br_ref, bi_ref, ab_ref, bb_ref):
    lr, li = lr_ref[...], li_ref[...]
    dt = jnp.exp(ldt_ref[...])
    mag = jnp.exp(lr * dt)
    ab_re, ab_im = mag * jnp.cos(li * dt), mag * jnp.sin(li * dt)
    den = lr * lr + li * li
    g_re = ((ab_re - 1.0) * lr + ab_im * li) / den
    g_im = (ab_im * lr - (ab_re - 1.0) * li) / den
    br, bi = br_ref[...], bi_ref[...]
    ab_ref[0:1, :] = ab_re
    ab_ref[1:2, :] = ab_im
    bb_ref[0] = g_re * br - g_im * bi
    bb_ref[1] = g_re * bi + g_im * br


def _s5_prep(lam_re, lam_im, log_dt, b_re, b_im):
    row = lambda z: z.reshape(1, S5_LANES)
    ldt = jnp.broadcast_to(log_dt[:, None], (S5_GROUPS, S5_STATE))
    tr = lambda b: b.transpose(2, 0, 1).reshape(S5_CH, S5_LANES)
    return pl.pallas_call(
        _s5_prep_kernel,
        out_shape=(jax.ShapeDtypeStruct((2, S5_LANES), F32),
                   jax.ShapeDtypeStruct((2, S5_CH, S5_LANES), F32)),
        name="s5_prep",
    )(row(lam_re), row(lam_im), row(ldt), tr(b_re), tr(b_im))


def _cmul(ar, ai, br, bi):
    return ar * br - ai * bi, ar * bi + ai * br


def _s5_kernel(u_ref, ab_ref, bmat_ref, cmat_ref, d_ref, gw_ref, gb_ref, o_ref,
               s_ref, tab_ref, car_ref, *, tc, cw):
    n = S5_LANES

    @pl.when(pl.program_id(0) == 0)
    def _():
        ar = jnp.broadcast_to(ab_ref[0:1, :], (SUBLANES, n))
        ai = jnp.broadcast_to(ab_ref[1:2, :], (SUBLANES, n))
        row = lax.broadcasted_iota(jnp.int32, (SUBLANES, n), 0)
        a2r, a2i = _cmul(ar, ai, ar, ai)
        a4r, a4i = _cmul(a2r, a2i, a2r, a2i)
        for k, (pr, pi, sh) in enumerate(((ar, ai, 1), (a2r, a2i, 2), (a4r, a4i, 4))):
            tab_ref[2 * k] = jnp.where(row >= sh, pr, 0.0)
            tab_ref[2 * k + 1] = jnp.where(row >= sh, pi, 0.0)
        pwr, pwi = ar, ai
        for bit, (qr, qi) in ((1, (ar, ai)), (2, (a2r, a2i)), (4, (a4r, a4i))):
            nr, ni = _cmul(pwr, pwi, qr, qi)
            sel = (row & bit) != 0
            pwr, pwi = jnp.where(sel, nr, pwr), jnp.where(sel, ni, pwi)
        tab_ref[6] = pwr
        tab_ref[7] = pwi
        car_ref[...] = jnp.zeros_like(car_ref)

    u = u_ref[...]
    s_ref[...] = _dot(u.astype(BF16), bmat_ref[...])

    for c in range(n // cw):
        cre = pl.ds(c * cw, cw)
        cim = pl.ds(n + c * cw, cw)
        t1r, t1i, t2r, t2i, t4r, t4i, pwr, pwi = (tab_ref[k, :, c * cw:(c + 1) * cw] for k in range(8))

        def body(i, carry):
            cr, ci = carry
            rows = pl.ds(pl.multiple_of(i * SUBLANES, SUBLANES), SUBLANES)
            xr, xi = s_ref[rows, cre], s_ref[rows, cim]
            for sh, tr_, ti_ in ((1, t1r, t1i), (2, t2r, t2i), (4, t4r, t4i)):
                dr, di = _cmul(tr_, ti_, pltpu.roll(xr, sh, 0), pltpu.roll(xi, sh, 0))
                xr, xi = xr + dr, xi + di
            dr, di = _cmul(pwr, pwi, jnp.broadcast_to(cr, xr.shape), jnp.broadcast_to(ci, xi.shape))
            xr, xi = xr + dr, xi + di
            s_ref[rows, cre] = xr
            s_ref[rows, cim] = xi
            return xr[SUBLANES - 1:SUBLANES, :], xi[SUBLANES - 1:SUBLANES, :]

        cr, ci = lax.fori_loop(0, tc // SUBLANES, body,
                               (car_ref[0:1, c * cw:(c + 1) * cw], car_ref[1:2, c * cw:(c + 1) * cw]),
                               unroll=2)
        car_ref[0:1, c * cw:(c + 1) * cw] = cr
        car_ref[1:2, c * cw:(c + 1) * cw] = ci

    y = _dot(s_ref[...].astype(BF16), cmat_ref[...]) + d_ref[...] * u
    v = _gelu_tanh(y)
    o_ref[...] = v * _sigmoid(_dot(v.astype(BF16), gw_ref[...]) + gb_ref[...])


def _s5(p, ab, bb, c_re, c_im, d, glu_w, glu_b, tc=512, cw=256):
    t = p.shape[0]
    n = S5_LANES
    grp_lane = jnp.arange(n) // S5_STATE
    grp_ch = jnp.arange(GROUP_W) // S5_CH
    same = grp_ch[:, None] == grp_lane[None, :]
    tile_b = lambda z: jnp.where(same, jnp.tile(z, (S5_GROUPS, 1)), 0.0)
    bmat = jnp.concatenate([tile_b(bb[0]), tile_b(bb[1])], axis=1).astype(BF16)
    tile_c = lambda z: jnp.where(same.T, jnp.tile(z.transpose(0, 2, 1).reshape(n, S5_CH), (1, S5_GROUPS)), 0.0)
    cmat = jnp.concatenate([tile_c(c_re), -tile_c(c_im)], axis=0).astype(BF16)
    const = lambda shape: pl.BlockSpec(shape, lambda i: (0,) * len(shape))
    return pl.pallas_call(
        functools.partial(_s5_kernel, tc=tc, cw=cw),
        grid=(t // tc,),
        in_specs=[
            pl.BlockSpec((tc, GROUP_W), lambda i: (i, 0)),
            const((2, n)), const((GROUP_W, 2 * n)), const((2 * n, GROUP_W)),
            const((1, GROUP_W)), const((GROUP_W, GROUP_W)), const((1, GROUP_W)),
        ],
        out_specs=pl.BlockSpec((tc, GROUP_W), lambda i: (i, 0)),
        out_shape=jax.ShapeDtypeStruct((t, GROUP_W), F32),
        scratch_shapes=[pltpu.VMEM((tc, 2 * n), F32),
                        pltpu.VMEM((8, SUBLANES, n), F32),
                        pltpu.VMEM((2, n), F32)],
        compiler_params=_cparams("arbitrary"),
        name="s5",
    )(p, ab, bmat, cmat, d.reshape(1, GROUP_W), glu_w.astype(BF16), glu_b.reshape(1, GROUP_W))


def _sb_kernel(q_ref, k_ref, v_ref, qg_ref, kg_ref, tri_ref, o_ref, kn_ref, vb_ref, qn_ref, acc_ref,
               z_ref, w_ref, *, tq, tk, ts):
    qi = pl.program_id(1)
    nfull = qi * (tq // tk)

    @pl.when(qi == 0)
    def _():
        kn_ref[...] = _rms(k_ref[...], kg_ref[...]).astype(BF16)
        vb_ref[...] = v_ref[...].astype(BF16)

    qn_ref[...] = (_rms(q_ref[...], qg_ref[...]) * (SB_HEAD_DIM ** -0.5)).astype(BF16)
    tri = tri_ref[...]
    diff = (lax.broadcasted_iota(jnp.int32, (ts, tk), 1) - lax.broadcasted_iota(jnp.int32, (ts, tk), 0))
    subs = tuple(range(0, tq, ts))
    krows = lambda tile: pl.ds(pl.multiple_of(tile * tk, tk), tk)

    def weights(z, carry, d):
        nl = jnp.maximum(z, 0.0) + jnp.log(1.0 + jnp.exp(-jnp.abs(z)))
        if d is not None:
            nl = jnp.where(diff < d, nl, 0.0)
        after = _split_dot(nl, tri)
        w = jnp.exp(z - nl - after - carry)
        if d is not None:
            w = jnp.where(diff < d, w, 0.0)
        return w.astype(BF16), carry + jnp.sum(nl, axis=-1, keepdims=True)

    carries = [jnp.zeros((ts, 1), F32) for _ in subs]
    acc = jnp.zeros((tq, SB_HEAD_DIM), F32)
    for kd in range(tq - tk, -1, -tk):
        rows = krows(nfull + kd // tk)
        pvs = []
        for n, r0 in enumerate(subs):
            d = r0 - kd
            if d <= -(ts - 1):
                pvs.append(jnp.zeros((ts, SB_HEAD_DIM), F32))
                continue
            z = _dot_nt(qn_ref[r0:r0 + ts, :], kn_ref[rows, :])
            w, carries[n] = weights(z, carries[n], d if d < tk else None)
            pvs.append(_dot(w, vb_ref[rows, :]))
        acc = acc + jnp.concatenate(pvs, axis=0)
    acc_ref[...] = acc

    z_ref[0] = _dot_nt(qn_ref[...], kn_ref[krows(jnp.maximum(nfull - 1, 0)), :])
    w_ref[0] = jnp.zeros((tq, tk), BF16)

    def body(it, cs):
        j = nfull - 1 - it
        cur = it % 2
        z_next = _dot_nt(qn_ref[...], kn_ref[krows(jnp.maximum(j - 1, 0)), :])
        pv = _dot(w_ref[cur], vb_ref[krows(j + 1), :])
        ws, new_cs = [], []
        for r0, c in zip(subs, cs):
            w, c = weights(z_ref[cur, r0:r0 + ts, :], c, None)
            ws.append(w)
            new_cs.append(c)
        w_ref[1 - cur] = jnp.concatenate(ws, axis=0)
        z_ref[1 - cur] = z_next
        acc_ref[...] += pv
        return tuple(new_cs)

    lax.fori_loop(0, nfull, body, tuple(carries))
    o_ref[...] = acc_ref[...] + _dot(w_ref[nfull % 2], vb_ref[krows(0), :])


def _sb(p, q_gain, k_gain, tq=512, tk=256, ts=128):
    t = p.shape[0]
    tri = jnp.tril(jnp.ones((tk, tk), F32), -1).astype(BF16)
    return pl.pallas_call(
        functools.partial(_sb_kernel, tq=tq, tk=tk, ts=ts),
        grid=(SB_HEADS, t // tq),
        in_specs=[
            pl.BlockSpec((tq, SB_HEAD_DIM), lambda h, i: (i, 4 + h)),
            pl.BlockSpec((t, SB_HEAD_DIM), lambda h, i: (0, 8 + h)),
            pl.BlockSpec((t, SB_HEAD_DIM), lambda h, i: (0, 12 + h)),
            pl.BlockSpec((1, SB_HEAD_DIM), lambda h, i: (0, 0)),
            pl.BlockSpec((1, SB_HEAD_DIM), lambda h, i: (0, 0)),
            pl.BlockSpec((tk, tk), lambda h, i: (0, 0)),
        ],
        out_specs=pl.BlockSpec((tq, SB_HEAD_DIM), lambda h, i: (i, h)),
        out_shape=jax.ShapeDtypeStruct((t, GROUP_W), F32),
        scratch_shapes=[pltpu.VMEM((t, SB_HEAD_DIM), BF16), pltpu.VMEM((t, SB_HEAD_DIM), BF16),
                        pltpu.VMEM((tq, SB_HEAD_DIM), BF16), pltpu.VMEM((tq, SB_HEAD_DIM), F32),
                        pltpu.VMEM((2, tq, tk), F32), pltpu.VMEM((2, tq, tk), BF16)],
        compiler_params=_cparams("arbitrary", "arbitrary"),
        name="stickbreak",
    )(p, p, p, q_gain.reshape(1, SB_HEAD_DIM), k_gain.reshape(1, SB_HEAD_DIM), tri)


RW_CHUNK = 64
(_MU_R, _MU_K, _MU_V, _MU_W, _MU_A, _MU_G, _W0, _A0, _KK, _KA, _RK, _LNG, _LNB) = range(13)


_BNN = (((2,), (1,)), ((0,), (0,)))
_BNT = (((2,), (2,)), ((0,), (0,)))


def _bmm(a, b, dims=_BNN):
    return lax.dot_general(a.astype(BF16), b.astype(BF16), dims, preferred_element_type=F32)


def _split_dot_left(m_bf16, x):
    hi = x.astype(BF16)
    lo = (x - hi.astype(F32)).astype(BF16)
    return _dot(m_bf16, hi) + _dot(m_bf16, lo)


def _rwkv_kernel(r_ref, k_ref, v_ref, m_ref, vec_ref, wa_ref, wb_ref, aa_ref, ab_ref, ga_ref, gb_ref,
                 ones_ref, tril_ref, blk_ref, o_ref, zs_ref, st_ref, *, tc):
    first = pl.program_id(0) == 0
    w = GROUP_W
    vec = lambda i: vec_ref[i:i + 1, :]

    @pl.when(first)
    def _():
        zs_ref[...] = jnp.zeros_like(zs_ref)
        st_ref[...] = jnp.zeros_like(st_ref)

    cur, prev = [], []
    for n, ref in enumerate((r_ref, k_ref, v_ref, m_ref)):
        zs_ref[n, SUBLANES - 1:SUBLANES, :] = zs_ref[n, SUBLANES + tc - 1:SUBLANES + tc, :]
        z = ref[...]
        zs_ref[n, SUBLANES:SUBLANES + tc, :] = z
        cur.append(z)
        prev.append(zs_ref[n, SUBLANES - 1:SUBLANES - 1 + tc, :])
    lerp = lambda n, mu: cur[n] + (prev[n] - cur[n]) * vec(mu)
    r, k, v = lerp(0, _MU_R), lerp(1, _MU_K), lerp(2, _MU_V)
    xw, xa, xg = lerp(3, _MU_W), lerp(3, _MU_A), lerp(3, _MU_G)

    lora = lambda x, a_ref, b_ref, f: _dot(f(_dot(x.astype(BF16), a_ref[...])).astype(BF16), b_ref[...])
    w_log = -_softplus(-(vec(_W0) + lora(xw, wa_ref, wb_ref, jnp.tanh))) - 0.5
    lw = -jnp.exp(w_log)
    a = _sigmoid(vec(_A0) + lora(xa, aa_ref, ab_ref, lambda z: z))
    g = lora(xg, ga_ref, gb_ref, _sigmoid)
    ones = ones_ref[...]
    kk = k * vec(_KK)
    kk = kk * lax.rsqrt(_split_dot(kk * kk, ones) + 1e-12)
    k = k * (1.0 + (a - 1.0) * vec(_KA))
    bonus = _split_dot(r * k * vec(_RK), ones) * v
    a_in, b_in = -kk, kk * a

    L = RW_CHUNK
    nch, nhp = tc // L, w // 128
    cum = _split_dot_left(tril_ref[...], lw)
    cum_l = _split_dot_left(blk_ref[...], lw)
    g_inv, g_end = jnp.exp(-cum), jnp.exp(cum_l - cum)
    head0 = lax.broadcasted_iota(jnp.int32, (L, 128), 1) < RW_HEAD_DIM

    def stacked(x, split_heads=True):
        out = []
        for c in range(nch):
            for hp in range(nhp):
                z = x[c * L:(c + 1) * L, hp * 128:(hp + 1) * 128]
                out.append(jnp.concatenate([jnp.where(head0, z, 0.0), jnp.where(head0, 0.0, z)], axis=0)
                           if split_heads else jnp.concatenate([z, z], axis=0))
        return jnp.stack(out)

    xa_, xr_ = stacked(a_in * jnp.exp(cum - lw)).astype(BF16), stacked(r * jnp.exp(cum)).astype(BF16)
    xb_, xk_, xv_ = stacked(b_in * g_inv).astype(BF16), stacked(k * g_inv).astype(BF16), stacked(v).astype(BF16)
    xbl_t = jnp.swapaxes(stacked(b_in * g_end), 1, 2).astype(BF16)
    xkl_t = jnp.swapaxes(stacked(k * g_end), 1, 2).astype(BF16)
    g_col = jnp.swapaxes(stacked(jnp.exp(cum_l), split_heads=False), 1, 2)

    ri = lax.broadcasted_iota(jnp.int32, (2 * L, 2 * L), 0)
    ci = lax.broadcasted_iota(jnp.int32, (2 * L, 2 * L), 1)
    strict, incl = ri > ci, ri >= ci
    eye = (ri == ci).astype(F32)
    pm = _bmm(jnp.concatenate([xa_, xr_], axis=1), jnp.concatenate([xb_, xk_], axis=1), _BNT)
    n_ab = jnp.where(strict, pm[:, :2 * L, :2 * L], 0.0)
    a_ak = jnp.where(strict, pm[:, :2 * L, 2 * L:], 0.0)
    a_rb = jnp.where(incl, pm[:, 2 * L:, :2 * L], 0.0).astype(BF16)
    a_rk = jnp.where(incl, pm[:, 2 * L:, 2 * L:], 0.0)
    tinv, npow = eye + n_ab, n_ab
    for _ in range(int(math.log2(L)) - 1):
        npow = _bmm(npow, npow)
        tinv = tinv + _bmm(npow, tinv)
    wu = _bmm(tinv, jnp.concatenate([xa_, _bmm(a_ak, xv_).astype(BF16)], axis=2))
    wx = jnp.concatenate([wu[:, :, :128].astype(BF16), xr_], axis=1)
    u0 = wu[:, :, 128:]
    y0 = _bmm(a_rk, xv_)
    s0 = _bmm(xkl_t, xv_)

    st = st_ref[...]
    ys = []
    for c in range(nch):
        sl = slice(c * nhp, (c + 1) * nhp)
        gs = _bmm(wx[sl], st)
        u = (gs[:, :2 * L] + u0[sl]).astype(BF16)
        yst = gs[:, 2 * L:] + _bmm(a_rb[sl], u) + y0[sl]
        st = g_col[sl] * st + _bmm(xbl_t[sl], u) + s0[sl]
        ys.append(jnp.concatenate([yst[hp, :L] + yst[hp, L:] for hp in range(nhp)], axis=1))
    st_ref[...] = st

    y = jnp.concatenate(ys, axis=0)
    inv_n = 1.0 / RW_HEAD_DIM
    mean = _split_dot(y, ones) * inv_n
    yc = y - mean
    var = _split_dot(yc * yc, ones) * inv_n
    y = yc * lax.rsqrt(var + RW_GN_EPS) * vec(_LNG) + vec(_LNB)
    o_ref[...] = (y + bonus) * g


def _rwkv(p, mu, w0, w_a, w_b, a0, a_a, a_b, g_a, g_b, k_k, k_a, r_k, ln_g, ln_b, tc=256):
    t = p.shape[0]
    w = GROUP_W
    vecs = jnp.concatenate([mu, jnp.stack([w0, a0, k_k, k_a, r_k.reshape(w), ln_g, ln_b]),
                            jnp.zeros((3, w), F32)], axis=0)
    head = jnp.arange(w) // RW_HEAD_DIM
    ones = (head[:, None] == head[None, :]).astype(BF16)
    chunk = jnp.arange(tc) // RW_CHUNK
    blk = chunk[:, None] == chunk[None, :]
    tril = jnp.logical_and(blk, jnp.arange(tc)[:, None] >= jnp.arange(tc)[None, :])
    const = lambda shape: pl.BlockSpec(shape, lambda i: (0,) * len(shape))
    col = lambda c: pl.BlockSpec((tc, w), lambda i: (i, c))
    bf = lambda z: z.astype(BF16)
    return pl.pallas_call(
        functools.partial(_rwkv_kernel, tc=tc),
        grid=(t // tc,),
        in_specs=[col(4), col(5), col(6), col(7), const((16, w)),
                  const(w_a.shape), const(w_b.shape), const(a_a.shape), const(a_b.shape),
                  const(g_a.shape), const(g_b.shape), const((w, w)), const((tc, tc)), const((tc, tc))],
        out_specs=pl.BlockSpec((tc, w), lambda i: (i, 0)),
        out_shape=jax.ShapeDtypeStruct((t, w), F32),
        scratch_shapes=[pltpu.VMEM((4, tc + SUBLANES, w), F32), pltpu.VMEM((w // 128, 128, 128), F32)],
        compiler_params=_cparams("arbitrary"),
        name="rwkv7",
    )(p, p, p, p, vecs, bf(w_a), bf(w_b), bf(a_a), bf(a_b), bf(g_a), bf(g_b), ones, bf(tril), bf(blk))


def _rglru_kernel(gate_ref, x_ref, cw_ref, vec_ref, wx_ref, wa_ref, o_ref,
                  xs_ref, a_s, b_s, car_ref, *, tc):
    first = pl.program_id(0) == 0
    halo = SUBLANES

    @pl.when(first)
    def _():
        xs_ref[...] = jnp.zeros_like(xs_ref)
        car_ref[...] = jnp.zeros_like(car_ref)

    xs_ref[0:halo, :] = xs_ref[tc:tc + halo, :]
    xs_ref[halo:halo + tc, :] = x_ref[...]
    xc = vec_ref[0:1, :]
    for j in range(RG_CONV):
        off = halo - (RG_CONV - 1) + j
        xc = xc + xs_ref[off:off + tc, :] * cw_ref[j:j + 1, :]
    xcb = xc.astype(BF16)
    gate_x = _sigmoid(_dot(xcb, wx_ref[...]) + vec_ref[1:2, :])
    gate_a = _sigmoid(_dot(xcb, wa_ref[...]) + vec_ref[2:3, :])
    lam = vec_ref[3:4, :]
    log_a = RG_C * gate_a * (-_softplus(-lam))
    a = jnp.exp(log_a)
    mult = jnp.sqrt(1.0 - a * a)
    t_idx = lax.broadcasted_iota(jnp.int32, (tc, GROUP_W), 0)
    mult = jnp.where(jnp.logical_and(first, t_idx == 0), 1.0, mult)
    a_s[...] = a
    b_s[...] = xc * gate_x * mult

    row = lax.broadcasted_iota(jnp.int32, (SUBLANES, GROUP_W), 0)

    def body(i, carry):
        rows = pl.ds(pl.multiple_of(i * SUBLANES, SUBLANES), SUBLANES)
        ab, hb = a_s[rows, :], b_s[rows, :]
        for sh in (1, 2, 4):
            ash = jnp.where(row >= sh, pltpu.roll(ab, sh, 0), 1.0)
            hsh = jnp.where(row >= sh, pltpu.roll(hb, sh, 0), 0.0)
            hb = hb + ab * hsh
            ab = ab * ash
        hb = hb + ab * jnp.broadcast_to(carry, hb.shape)
        b_s[rows, :] = hb
        return hb[SUBLANES - 1:SUBLANES, :]

    car_ref[...] = lax.fori_loop(0, tc // SUBLANES, body, car_ref[...], unroll=2)
    o_ref[...] = _gelu_tanh(gate_ref[...]) * b_s[...]


def _rglru(p, conv_w, conv_b, w_x, b_x, w_a, b_a, lam, tc=512):
    t = p.shape[0]
    w = GROUP_W
    blk = jnp.arange(w) // (w // w_x.shape[0])
    same = blk[:, None] == blk[None, :]
    bdiag = lambda m: jnp.where(same, jnp.tile(m.reshape(w, -1), (1, m.shape[0])), 0.0).astype(BF16)
    vecs = jnp.concatenate([jnp.stack([conv_b, b_x, b_a, lam]), jnp.zeros((4, w), F32)], axis=0)
    const = lambda shape: pl.BlockSpec(shape, lambda i: (0,) * len(shape))
    return pl.pallas_call(
        functools.partial(_rglru_kernel, tc=tc),
        grid=(t // tc,),
        in_specs=[pl.BlockSpec((tc, w), lambda i: (i, 8)), pl.BlockSpec((tc, w), lambda i: (i, 9)),
                  const((RG_CONV, w)), const((8, w)), const((w, w)), const((w, w))],
        out_specs=pl.BlockSpec((tc, w), lambda i: (i, 0)),
        out_shape=jax.ShapeDtypeStruct((t, w), F32),
        scratch_shapes=[pltpu.VMEM((tc + SUBLANES, w), F32), pltpu.VMEM((tc, w), F32),
                        pltpu.VMEM((tc, w), F32), pltpu.VMEM((1, w), F32)],
        compiler_params=_cparams("arbitrary"),
        name="rglru",
    )(p, p, conv_w, vecs, bdiag(w_x), bdiag(w_a))


def kernel(x, ffn1_norm, ffn1_w1, ffn1_w3, ffn1_w2, mix_norm, w_in, s5_lambda_re, s5_lambda_im, s5_log_dt, s5_b_re, s5_b_im, s5_c_re, s5_c_im, s5_d, s5_glu_w, s5_glu_b, sb_q_gain, sb_k_gain, rw_mu, rw_w0, rw_w_a, rw_w_b, rw_a0, rw_a_a, rw_a_b, rw_g_a, rw_g_b, rw_k_k, rw_k_a, rw_r_k, rw_ln_g, rw_ln_b, rg_conv_w, rg_conv_b, rg_w_x, rg_b_x, rg_w_a, rg_b_a, rg_lambda, out_norm, w_out, ffn2_norm, ffn2_w1, ffn2_w3, ffn2_w2):
    bn, t, _ = x.shape
    assert bn == 1
    x = x.reshape(t, D_MODEL)
    bf = lambda z: z.astype(BF16)
    for l in range(ffn1_norm.shape[0]):
        x = _ffn(x, ffn1_norm[l], bf(ffn1_w1[l]), bf(ffn1_w3[l]), bf(ffn1_w2[l]))
        p = _inproj(x, mix_norm[l], bf(w_in[l]))
        ab, bb = _s5_prep(s5_lambda_re[l], s5_lambda_im[l], s5_log_dt[l], s5_b_re[l], s5_b_im[l])
        y_a = _s5(p, ab, bb, s5_c_re[l], s5_c_im[l], s5_d[l], s5_glu_w[l], s5_glu_b[l])
        y_b = _sb(p, sb_q_gain[l], sb_k_gain[l])
        y_c = _rwkv(p, rw_mu[l], rw_w0[l], rw_w_a[l], rw_w_b[l], rw_a0[l], rw_a_a[l], rw_a_b[l],
                    rw_g_a[l], rw_g_b[l], rw_k_k[l], rw_k_a[l], rw_r_k[l], rw_ln_g[l], rw_ln_b[l])
        y_d = _rglru(p, rg_conv_w[l], rg_conv_b[l], rg_w_x[l], rg_b_x[l], rg_w_a[l], rg_b_a[l],
                     rg_lambda[l])
        x = _outproj(x, (y_a, y_b, y_c, y_d), out_norm[l], bf(w_out[l]))
        x = _ffn(x, ffn2_norm[l], bf(ffn2_w1[l]), bf(ffn2_w3[l]), bf(ffn2_w2[l]))
    return x.reshape(bn, t, D_MODEL)
```

```python
import functools
import math

import jax
import jax.numpy as jnp
from jax import lax
from jax.experimental import pallas as pl
from jax.experimental.pallas import tpu as pltpu

F32 = jnp.float32
BF16 = jnp.bfloat16

D_MODEL = 2048
GROUP_W = 512
D_FF = 5632
D_IN = 10 * GROUP_W
S5_CH = 16
S5_GROUPS = 32
S5_STATE = 64
S5_LANES = S5_GROUPS * S5_STATE
SB_HEADS = 4
SB_HEAD_DIM = 128
RW_HEAD_DIM = 64
RW_GN_EPS = 64e-5
RG_CONV = 4
RG_C = 8.0
NORM_EPS = 1e-6
LOG2E = 1.4426950408889634

SUBLANES = 8
VMEM_LIMIT = 56 * 1024 * 1024

HI = lax.Precision.HIGHEST


def _cparams(*sem):
    return pltpu.CompilerParams(dimension_semantics=sem, vmem_limit_bytes=VMEM_LIMIT)


def _dot(a, b):
    return jnp.dot(a, b, preferred_element_type=F32)


def _dot_nt(a, b):
    return lax.dot_general(a, b, (((1,), (1,)), ((), ())), preferred_element_type=F32)


def _split_dot(x, m_bf16):
    hi = x.astype(BF16)
    lo = (x - hi.astype(F32)).astype(BF16)
    return _dot(hi, m_bf16) + _dot(lo, m_bf16)


def _rms(x, g):
    return x * lax.rsqrt(jnp.mean(x * x, axis=-1, keepdims=True) + NORM_EPS) * g


def _sigmoid(x):
    return 1.0 / (1.0 + jnp.exp(-x))


def _softplus(x):
    return jnp.maximum(x, 0.0) + jnp.log1p(jnp.exp(-jnp.abs(x)))


def _gelu_tanh(x):
    c = math.sqrt(2.0 / math.pi)
    return 0.5 * x * (1.0 + jnp.tanh(c * (x + 0.044715 * (x * x * x))))


def _ffn_kernel(x_ref, g_ref, w1_ref, w3_ref, w2_ref, o_ref, h_ref):
    @pl.when(pl.program_id(1) == 0)
    def _():
        x = x_ref[...]
        h_ref[...] = _rms(x, g_ref[...]).astype(BF16)
        o_ref[...] = x

    h = h_ref[...]
    a = _dot(h, w1_ref[...])
    b = _dot(h, w3_ref[...])
    act = (0.5 * (a * _sigmoid(a)) * b).astype(BF16)
    o_ref[...] += _dot(act, w2_ref[...])


def _ffn(x, g, w1, w3, w2, tm=1024, tf=512):
    t = x.shape[0]
    return pl.pallas_call(
        _ffn_kernel,
        grid=(t // tm, D_FF // tf),
        in_specs=[
            pl.BlockSpec((tm, D_MODEL), lambda i, j: (i, 0), pipeline_mode=pl.Buffered(1)),
            pl.BlockSpec((1, D_MODEL), lambda i, j: (0, 0)),
            pl.BlockSpec((D_MODEL, tf), lambda i, j: (0, j)),
            pl.BlockSpec((D_MODEL, tf), lambda i, j: (0, j)),
            pl.BlockSpec((tf, D_MODEL), lambda i, j: (j, 0)),
        ],
        out_specs=pl.BlockSpec((tm, D_MODEL), lambda i, j: (i, 0)),
        out_shape=jax.ShapeDtypeStruct((t, D_MODEL), F32),
        scratch_shapes=[pltpu.VMEM((tm, D_MODEL), BF16)],
        compiler_params=_cparams("parallel", "arbitrary"),
        name="ffn",
    )(x, g.reshape(1, D_MODEL), w1, w3, w2)


def _inproj_kernel(x_ref, g_ref, w_ref, o_ref, h_ref):
    @pl.when(pl.program_id(1) == 0)
    def _():
        h_ref[...] = _rms(x_ref[...], g_ref[...]).astype(BF16)

    o_ref[...] = _dot(h_ref[...], w_ref[...])


def _inproj(x, g, w, tm=1024, tn=1024):
    t = x.shape[0]
    return pl.pallas_call(
        _inproj_kernel,
        grid=(t // tm, D_IN // tn),
        in_specs=[
            pl.BlockSpec((tm, D_MODEL), lambda i, j: (i, 0), pipeline_mode=pl.Buffered(1)),
            pl.BlockSpec((1, D_MODEL), lambda i, j: (0, 0)),
            pl.BlockSpec((D_MODEL, tn), lambda i, j: (0, j)),
        ],
        out_specs=pl.BlockSpec((tm, tn), lambda i, j: (i, j)),
        out_shape=jax.ShapeDtypeStruct((t, D_IN), F32),
        scratch_shapes=[pltpu.VMEM((tm, D_MODEL), BF16)],
        compiler_params=_cparams("parallel", "arbitrary"),
        name="inproj",
    )(x, g.reshape(1, D_MODEL), w)


def _outproj_kernel(x_ref, ya_ref, yb_ref, yc_ref, yd_ref, g_ref, w_ref, o_ref):
    acc = x_ref[...]
    for n, y_ref in enumerate((ya_ref, yb_ref, yc_ref, yd_ref)):
        yn = _rms(y_ref[...], g_ref[n:n + 1, :]).astype(BF16)
        acc = acc + _dot(yn, w_ref[n * GROUP_W:(n + 1) * GROUP_W, :])
    o_ref[...] = acc


def _outproj(x, ys, g, w, tm=512):
    t = x.shape[0]
    yspec = pl.BlockSpec((tm, GROUP_W), lambda i: (i, 0))
    return pl.pallas_call(
        _outproj_kernel,
        grid=(t // tm,),
        in_specs=[
            pl.BlockSpec((tm, D_MODEL), lambda i: (i, 0)),
            yspec, yspec, yspec, yspec,
            pl.BlockSpec((4, GROUP_W), lambda i: (0, 0)),
            pl.BlockSpec((D_MODEL, D_MODEL), lambda i: (0, 0)),
        ],
        out_specs=pl.BlockSpec((tm, D_MODEL), lambda i: (i, 0)),
        out_shape=jax.ShapeDtypeStruct((t, D_MODEL), F32),
        compiler_params=_cparams("parallel"),
        name="outproj",
    )(x, *ys, g.reshape(4, GROUP_W), w)


def _s5_prep_kernel(lr_ref, li_ref, ldt_ref, br_ref, bi_ref, ab_ref, bb_ref):
    lr, li = lr_ref[...], li_ref[...]
    dt = jnp.exp(ldt_ref[...])
    mag = jnp.exp(lr * dt)
    ab_re, ab_im = mag * jnp.cos(li * dt), mag * jnp.sin(li * dt)
    den = lr * lr + li * li
    g_re = ((ab_re - 1.0) * lr + ab_im * li) / den
    g_im = (ab_im * lr - (ab_re - 1.0) * li) / den
    br, bi = br_ref[...], bi_ref[...]
    ab_ref[0:1, :] = ab_re
    ab_ref[1:2, :] = ab_im
    bb_ref[0] = g_re * br - g_im * bi
    bb_ref[1] = g_re * bi + g_im * br


def _s5_prep(lam_re, lam_im, log_dt, b_re, b_im):
    row = lambda z: z.reshape(1, S5_LANES)
    ldt = jnp.broadcast_to(log_dt[:, None], (S5_GROUPS, S5_STATE))
    tr = lambda b: b.transpose(2, 0, 1).reshape(S5_CH, S5_LANES)
    return pl.pallas_call(
        _s5_prep_kernel,
        out_shape=(jax.ShapeDtypeStruct((2, S5_LANES), F32),
                   jax.ShapeDtypeStruct((2, S5_CH, S5_LANES), F32)),
        name="s5_prep",
    )(row(lam_re), row(lam_im), row(ldt), tr(b_re), tr(b_im))


def _cmul(ar, ai, br, bi):
    return ar * br - ai * bi, ar * bi + ai * br


def _s5_kernel(u_ref, ab_ref, bmat_ref, cmat_ref, d_ref, gw_ref, gb_ref, o_ref,
               s_ref, tab_ref, car_ref, *, tc, cw):
    n = S5_LANES

    @pl.when(pl.program_id(0) == 0)
    def _():
        ar = jnp.broadcast_to(ab_ref[0:1, :], (SUBLANES, n))
        ai = jnp.broadcast_to(ab_ref[1:2, :], (SUBLANES, n))
        row = lax.broadcasted_iota(jnp.int32, (SUBLANES, n), 0)
        a2r, a2i = _cmul(ar, ai, ar, ai)
        a4r, a4i = _cmul(a2r, a2i, a2r, a2i)
        for k, (pr, pi, sh) in enumerate(((ar, ai, 1), (a2r, a2i, 2), (a4r, a4i, 4))):
            tab_ref[2 * k] = jnp.where(row >= sh, pr, 0.0)
            tab_ref[2 * k + 1] = jnp.where(row >= sh, pi, 0.0)
        pwr, pwi = ar, ai
        for bit, (qr, qi) in ((1, (ar, ai)), (2, (a2r, a2i)), (4, (a4r, a4i))):
            nr, ni = _cmul(pwr, pwi, qr, qi)
            sel = (row & bit) != 0
            pwr, pwi = jnp.where(sel, nr, pwr), jnp.where(sel, ni, pwi)
        tab_ref[6] = pwr
        tab_ref[7] = pwi
        car_ref[...] = jnp.zeros_like(car_ref)

    u = u_ref[...]
    s_ref[...] = _dot(u.astype(BF16), bmat_ref[...])

    for c in range(n // cw):
        cre = pl.ds(c * cw, cw)
        cim = pl.ds(n + c * cw, cw)
        t1r, t1i, t2r, t2i, t4r, t4i, pwr, pwi = (tab_ref[k, :, c * cw:(c + 1) * cw] for k in range(8))

        def body(i, carry):
            cr, ci = carry
            rows = pl.ds(pl.multiple_of(i * SUBLANES, SUBLANES), SUBLANES)
            xr, xi = s_ref[rows, cre], s_ref[rows, cim]
            for sh, tr_, ti_ in ((1, t1r, t1i), (2, t2r, t2i), (4, t4r, t4i)):
                dr, di = _cmul(tr_, ti_, pltpu.roll(xr, sh, 0), pltpu.roll(xi, sh, 0))
                xr, xi = xr + dr, xi + di
            dr, di = _cmul(pwr, pwi, jnp.broadcast_to(cr, xr.shape), jnp.broadcast_to(ci, xi.shape))
            xr, xi = xr + dr, xi + di
            s_ref[rows, cre] = xr
            s_ref[rows, cim] = xi
            return xr[SUBLANES - 1:SUBLANES, :], xi[SUBLANES - 1:SUBLANES, :]

        cr, ci = lax.fori_loop(0, tc // SUBLANES, body,
                               (car_ref[0:1, c * cw:(c + 1) * cw], car_ref[1:2, c * cw:(c + 1) * cw]),
                               unroll=2)
        car_ref[0:1, c * cw:(c + 1) * cw] = cr
        car_ref[1:2, c * cw:(c + 1) * cw] = ci

    y = _dot(s_ref[...].astype(BF16), cmat_ref[...]) + d_ref[...] * u
    v = _gelu_tanh(y)
    o_ref[...] = v * _sigmoid(_dot(v.astype(BF16), gw_ref[...]) + gb_ref[...])


def _s5(p, ab, bb, c_re, c_im, d, glu_w, glu_b, tc=512, cw=256):
    t = p.shape[0]
    n = S5_LANES
    grp_lane = jnp.arange(n) // S5_STATE
    grp_ch = jnp.arange(GROUP_W) // S5_CH
    same = grp_ch[:, None] == grp_lane[None, :]
    tile_b = lambda z: jnp.where(same, jnp.tile(z, (S5_GROUPS, 1)), 0.0)
    bmat = jnp.concatenate([tile_b(bb[0]), tile_b(bb[1])], axis=1).astype(BF16)
    tile_c = lambda z: jnp.where(same.T, jnp.tile(z.transpose(0, 2, 1).reshape(n, S5_CH), (1, S5_GROUPS)), 0.0)
    cmat = jnp.concatenate([tile_c(c_re), -tile_c(c_im)], axis=0).astype(BF16)
    const = lambda shape: pl.BlockSpec(shape, lambda i: (0,) * len(shape))
    return pl.pallas_call(
        functools.partial(_s5_kernel, tc=tc, cw=cw),
        grid=(t // tc,),
        in_specs=[
            pl.BlockSpec((tc, GROUP_W), lambda i: (i, 0)),
            const((2, n)), const((GROUP_W, 2 * n)), const((2 * n, GROUP_W)),
            const((1, GROUP_W)), const((GROUP_W, GROUP_W)), const((1, GROUP_W)),
        ],
        out_specs=pl.BlockSpec((tc, GROUP_W), lambda i: (i, 0)),
        out_shape=jax.ShapeDtypeStruct((t, GROUP_W), F32),
        scratch_shapes=[pltpu.VMEM((tc, 2 * n), F32),
                        pltpu.VMEM((8, SUBLANES, n), F32),
                        pltpu.VMEM((2, n), F32)],
        compiler_params=_cparams("arbitrary"),
        name="s5",
    )(p, ab, bmat, cmat, d.reshape(1, GROUP_W), glu_w.astype(BF16), glu_b.reshape(1, GROUP_W))


def _sb_kernel(q_ref, k_ref, v_ref, qg_ref, kg_ref, tri_ref, o_ref, kn_ref, vb_ref, qn_ref, acc_ref,
               z_ref, d_ref, after_ref, w_ref, *, tq, tk, ts):
    qi = pl.program_id(1)
    nd = tq // tk
    nfull = qi * nd

    @pl.when(qi == 0)
    def _():
        kn_ref[...] = _rms(k_ref[...], kg_ref[...]).astype(BF16)
        vb_ref[...] = v_ref[...].astype(BF16)

    qn_ref[...] = (_rms(q_ref[...], qg_ref[...]) * (SB_HEAD_DIM ** -0.5)).astype(BF16)
    tri = tri_ref[...]
    diff = (lax.broadcasted_iota(jnp.int32, (ts, tk), 1) - lax.broadcasted_iota(jnp.int32, (ts, tk), 0))
    subs = tuple(range(0, tq, ts))
    krows = lambda tile: pl.ds(pl.multiple_of(tile * tk, tk), tk)

    def scores(tile):
        return _dot_nt(qn_ref[...], kn_ref[krows(tile), :])

    ntiles = nd + nfull
    hidden = -1e30

    def step(it, par, state, ds, do_q=True, do_n=True, do_w=True, do_p=True):
        cs, rs = state
        if do_q:
            z_next = scores(jnp.maximum(ntiles - 2 - it, 0))
        if do_p:
            pv = _dot(w_ref[par], vb_ref[krows(ntiles + 1 - it), :])
        if do_n:
            dvs, afters, new_rs = [], [], []
            for r0, d in zip(subs, ds):
                if d is False:
                    dvs.append(jnp.full((ts, tk), hidden, F32))
                    afters.append(jnp.zeros((ts, tk), F32))
                    new_rs.append(jnp.zeros((ts, 1), F32))
                    continue
                z = z_ref[par, r0:r0 + ts, :]
                nl = jnp.maximum(z, 0.0) + jnp.log(1.0 + jnp.exp2(jnp.abs(z) * (-LOG2E)))
                dv = z - nl
                if d is not None:
                    nl, dv = jnp.where(diff < d, nl, 0.0), jnp.where(diff < d, dv, hidden)
                dvs.append(dv)
                afters.append(_dot(nl.astype(BF16), tri))
                new_rs.append(jnp.sum(nl, axis=-1, keepdims=True))
            d_ref[par] = jnp.concatenate(dvs, axis=0)
        if do_w:
            ws = [jnp.exp(d_ref[1 - par, r0:r0 + ts, :] - after_ref[1 - par, r0:r0 + ts, :] - c).astype(BF16)
                  for r0, c in zip(subs, cs)]
            w_ref[1 - par] = jnp.concatenate(ws, axis=0)
            cs = tuple(c + r for c, r in zip(cs, rs))
        if do_n:
            after_ref[par] = jnp.concatenate(afters, axis=0)
            rs = tuple(new_rs)
        if do_q:
            z_ref[1 - par] = z_next
        if do_p:
            acc_ref[...] += pv
        return cs, rs

    z_ref[0] = scores(ntiles - 1)
    acc_ref[...] = jnp.zeros_like(acc_ref)
    zeros = tuple(jnp.zeros((ts, 1), F32) for _ in subs)
    state = (zeros, zeros)
    for it in range(nd):
        kd = (nd - 1 - it) * tk
        ds = [False if r0 - kd <= -(ts - 1) else (None if r0 - kd >= tk else r0 - kd) for r0 in subs]
        state = step(it, it % 2, state, ds, do_w=it >= 1, do_p=it >= 2)

    def two_trips(n, st):
        for par in range(2):
            st = step(nd + 2 * n + par, par, st, [None] * len(subs))
        return st

    state = lax.fori_loop(0, nfull // 2, two_trips, state)
    step(ntiles, 0, state, None, do_q=False, do_n=False)
    o_ref[...] = acc_ref[...] + _dot(w_ref[1], vb_ref[krows(0), :])


def _sb(p, q_gain, k_gain, tq=512, tk=256, ts=128):
    t = p.shape[0]
    tri = jnp.tril(jnp.ones((tk, tk), F32), -1).astype(BF16)
    return pl.pallas_call(
        functools.partial(_sb_kernel, tq=tq, tk=tk, ts=ts),
        grid=(SB_HEADS, t // tq),
        in_specs=[
            pl.BlockSpec((tq, SB_HEAD_DIM), lambda h, i: (i, 4 + h)),
            pl.BlockSpec((t, SB_HEAD_DIM), lambda h, i: (0, 8 + h)),
            pl.BlockSpec((t, SB_HEAD_DIM), lambda h, i: (0, 12 + h)),
            pl.BlockSpec((1, SB_HEAD_DIM), lambda h, i: (0, 0)),
            pl.BlockSpec((1, SB_HEAD_DIM), lambda h, i: (0, 0)),
            pl.BlockSpec((tk, tk), lambda h, i: (0, 0)),
        ],
        out_specs=pl.BlockSpec((tq, SB_HEAD_DIM), lambda h, i: (i, h)),
        out_shape=jax.ShapeDtypeStruct((t, GROUP_W), F32),
        scratch_shapes=[pltpu.VMEM((t, SB_HEAD_DIM), BF16), pltpu.VMEM((t, SB_HEAD_DIM), BF16),
                        pltpu.VMEM((tq, SB_HEAD_DIM), BF16), pltpu.VMEM((tq, SB_HEAD_DIM), F32),
                        pltpu.VMEM((2, tq, tk), F32), pltpu.VMEM((2, tq, tk), F32),
                        pltpu.VMEM((2, tq, tk), F32), pltpu.VMEM((2, tq, tk), BF16)],
        compiler_params=_cparams("arbitrary", "arbitrary"),
        name="stickbreak",
    )(p, p, p, q_gain.reshape(1, SB_HEAD_DIM), k_gain.reshape(1, SB_HEAD_DIM), tri)


RW_CHUNK = 64
(_MU_R, _MU_K, _MU_V, _MU_W, _MU_A, _MU_G, _W0, _A0, _KK, _KA, _RK, _LNG, _LNB) = range(13)


_BNN = (((2,), (1,)), ((0,), (0,)))
_BNT = (((2,), (2,)), ((0,), (0,)))


def _bmm(a, b, dims=_BNN):
    return lax.dot_general(a.astype(BF16), b.astype(BF16), dims, preferred_element_type=F32)


def _split_dot_left(m_bf16, x):
    hi = x.astype(BF16)
    lo = (x - hi.astype(F32)).astype(BF16)
    return _dot(m_bf16, hi) + _dot(m_bf16, lo)


def _rwkv_kernel(r_ref, k_ref, v_ref, m_ref, vec_ref, wa_ref, wb_ref, aa_ref, ab_ref, ga_ref, gb_ref,
                 ones_ref, tril_ref, blk_ref, o_ref, zs_ref, st_ref, *, tc):
    first = pl.program_id(0) == 0
    w = GROUP_W
    vec = lambda i: vec_ref[i:i + 1, :]

    @pl.when(first)
    def _():
        zs_ref[...] = jnp.zeros_like(zs_ref)
        st_ref[...] = jnp.zeros_like(st_ref)

    cur, prev = [], []
    for n, ref in enumerate((r_ref, k_ref, v_ref, m_ref)):
        zs_ref[n, SUBLANES - 1:SUBLANES, :] = zs_ref[n, SUBLANES + tc - 1:SUBLANES + tc, :]
        z = ref[...]
        zs_ref[n, SUBLANES:SUBLANES + tc, :] = z
        cur.append(z)
        prev.append(zs_ref[n, SUBLANES - 1:SUBLANES - 1 + tc, :])
    lerp = lambda n, mu: cur[n] + (prev[n] - cur[n]) * vec(mu)
    r, k, v = lerp(0, _MU_R), lerp(1, _MU_K), lerp(2, _MU_V)
    xw, xa, xg = lerp(3, _MU_W), lerp(3, _MU_A), lerp(3, _MU_G)

    lora = lambda x, a_ref, b_ref, f: _dot(f(_dot(x.astype(BF16), a_ref[...])).astype(BF16), b_ref[...])
    w_log = -_softplus(-(vec(_W0) + lora(xw, wa_ref, wb_ref, jnp.tanh))) - 0.5
    lw = -jnp.exp(w_log)
    a = _sigmoid(vec(_A0) + lora(xa, aa_ref, ab_ref, lambda z: z))
    g = lora(xg, ga_ref, gb_ref, _sigmoid)
    ones = ones_ref[...]
    kk = k * vec(_KK)
    kk = kk * lax.rsqrt(_split_dot(kk * kk, ones) + 1e-12)
    k = k * (1.0 + (a - 1.0) * vec(_KA))
    bonus = _split_dot(r * k * vec(_RK), ones) * v
    a_in, b_in = -kk, kk * a

    L = RW_CHUNK
    nch, nhp = tc // L, w // 128
    cum = _split_dot_left(tril_ref[...], lw)
    cum_l = _split_dot_left(blk_ref[...], lw)
    g_inv, g_end = jnp.exp(-cum), jnp.exp(cum_l - cum)
    head0 = lax.broadcasted_iota(jnp.int32, (L, 128), 1) < RW_HEAD_DIM

    def stacked(x, split_heads=True):
        out = []
        for c in range(nch):
            for hp in range(nhp):
                z = x[c * L:(c + 1) * L, hp * 128:(hp + 1) * 128]
                out.append(jnp.concatenate([jnp.where(head0, z, 0.0), jnp.where(head0, 0.0, z)], axis=0)
                           if split_heads else jnp.concatenate([z, z], axis=0))
        return jnp.stack(out)

    xa_, xr_ = stacked(a_in * jnp.exp(cum - lw)).astype(BF16), stacked(r * jnp.exp(cum)).astype(BF16)
    xb_, xk_, xv_ = stacked(b_in * g_inv).astype(BF16), stacked(k * g_inv).astype(BF16), stacked(v).astype(BF16)
    xbl_t = jnp.swapaxes(stacked(b_in * g_end), 1, 2).astype(BF16)
    xkl_t = jnp.swapaxes(stacked(k * g_end), 1, 2).astype(BF16)
    g_col = jnp.swapaxes(stacked(jnp.exp(cum_l), split_heads=False), 1, 2)

    ri = lax.broadcasted_iota(jnp.int32, (2 * L, 2 * L), 0)
    ci = lax.broadcasted_iota(jnp.int32, (2 * L, 2 * L), 1)
    strict, incl = ri > ci, ri >= ci
    eye = (ri == ci).astype(F32)
    pm = _bmm(jnp.concatenate([xa_, xr_], axis=1), jnp.concatenate([xb_, xk_], axis=1), _BNT)
    n_ab = jnp.where(strict, pm[:, :2 * L, :2 * L], 0.0)
    a_ak = jnp.where(strict, pm[:, :2 * L, 2 * L:], 0.0)
    a_rb = jnp.where(incl, pm[:, 2 * L:, :2 * L], 0.0).astype(BF16)
    a_rk = jnp.where(incl, pm[:, 2 * L:, 2 * L:], 0.0)
    tinv, npow = eye + n_ab, n_ab
    for _ in range(int(math.log2(L)) - 1):
        npow = _bmm(npow, npow)
        tinv = tinv + _bmm(npow, tinv)
    wu = _bmm(tinv, jnp.concatenate([xa_, _bmm(a_ak, xv_).astype(BF16)], axis=2))
    wx = jnp.concatenate([wu[:, :, :128].astype(BF16), xr_], axis=1)
    u0 = wu[:, :, 128:]
    y0 = _bmm(a_rk, xv_)
    s0 = _bmm(xkl_t, xv_)

    st = st_ref[...]
    ys = []
    for c in range(nch):
        sl = slice(c * nhp, (c + 1) * nhp)
        gs = _bmm(wx[sl], st)
        u = (gs[:, :2 * L] + u0[sl]).astype(BF16)
        yst = gs[:, 2 * L:] + _bmm(a_rb[sl], u) + y0[sl]
        st = g_col[sl] * st + _bmm(xbl_t[sl], u) + s0[sl]
        ys.append(jnp.concatenate([yst[hp, :L] + yst[hp, L:] for hp in range(nhp)], axis=1))
    st_ref[...] = st

    y = jnp.concatenate(ys, axis=0)
    inv_n = 1.0 / RW_HEAD_DIM
    mean = _split_dot(y, ones) * inv_n
    yc = y - mean
    var = _split_dot(yc * yc, ones) * inv_n
    y = yc * lax.rsqrt(var + RW_GN_EPS) * vec(_LNG) + vec(_LNB)
    o_ref[...] = (y + bonus) * g


def _rwkv(p, mu, w0, w_a, w_b, a0, a_a, a_b, g_a, g_b, k_k, k_a, r_k, ln_g, ln_b, tc=256):
    t = p.shape[0]
    w = GROUP_W
    vecs = jnp.concatenate([mu, jnp.stack([w0, a0, k_k, k_a, r_k.reshape(w), ln_g, ln_b]),
                            jnp.zeros((3, w), F32)], axis=0)
    head = jnp.arange(w) // RW_HEAD_DIM
    ones = (head[:, None] == head[None, :]).astype(BF16)
    chunk = jnp.arange(tc) // RW_CHUNK
    blk = chunk[:, None] == chunk[None, :]
    tril = jnp.logical_and(blk, jnp.arange(tc)[:, None] >= jnp.arange(tc)[None, :])
    const = lambda shape: pl.BlockSpec(shape, lambda i: (0,) * len(shape))
    col = lambda c: pl.BlockSpec((tc, w), lambda i: (i, c))
    bf = lambda z: z.astype(BF16)
    return pl.pallas_call(
        functools.partial(_rwkv_kernel, tc=tc),
        grid=(t // tc,),
        in_specs=[col(4), col(5), col(6), col(7), const((16, w)),
                  const(w_a.shape), const(w_b.shape), const(a_a.shape), const(a_b.shape),
                  const(g_a.shape), const(g_b.shape), const((w, w)), const((tc, tc)), const((tc, tc))],
        out_specs=pl.BlockSpec((tc, w), lambda i: (i, 0)),
        out_shape=jax.ShapeDtypeStruct((t, w), F32),
        scratch_shapes=[pltpu.VMEM((4, tc + SUBLANES, w), F32), pltpu.VMEM((w // 128, 128, 128), F32)],
        compiler_params=_cparams("arbitrary"),
        name="rwkv7",
    )(p, p, p, p, vecs, bf(w_a), bf(w_b), bf(a_a), bf(a_b), bf(g_a), bf(g_b), ones, bf(tril), bf(blk))


def _rglru_kernel(gate_ref, x_ref, cw_ref, vec_ref, wx_ref, wa_ref, o_ref,
                  xs_ref, a_s, b_s, car_ref, *, tc):
    first = pl.program_id(0) == 0
    halo = SUBLANES

    @pl.when(first)
    def _():
        xs_ref[...] = jnp.zeros_like(xs_ref)
        car_ref[...] = jnp.zeros_like(car_ref)

    xs_ref[0:halo, :] = xs_ref[tc:tc + halo, :]
    xs_ref[halo:halo + tc, :] = x_ref[...]
    xc = vec_ref[0:1, :]
    for j in range(RG_CONV):
        off = halo - (RG_CONV - 1) + j
        xc = xc + xs_ref[off:off + tc, :] * cw_ref[j:j + 1, :]
    xcb = xc.astype(BF16)
    gate_x = _sigmoid(_dot(xcb, wx_ref[...]) + vec_ref[1:2, :])
    gate_a = _sigmoid(_dot(xcb, wa_ref[...]) + vec_ref[2:3, :])
    lam = vec_ref[3:4, :]
    log_a = RG_C * gate_a * (-_softplus(-lam))
    a = jnp.exp(log_a)
    mult = jnp.sqrt(1.0 - a * a)
    t_idx = lax.broadcasted_iota(jnp.int32, (tc, GROUP_W), 0)
    mult = jnp.where(jnp.logical_and(first, t_idx == 0), 1.0, mult)
    a_s[...] = a
    b_s[...] = xc * gate_x * mult

    row = lax.broadcasted_iota(jnp.int32, (SUBLANES, GROUP_W), 0)

    def body(i, carry):
        rows = pl.ds(pl.multiple_of(i * SUBLANES, SUBLANES), SUBLANES)
        ab, hb = a_s[rows, :], b_s[rows, :]
        for sh in (1, 2, 4):
            ash = jnp.where(row >= sh, pltpu.roll(ab, sh, 0), 1.0)
            hsh = jnp.where(row >= sh, pltpu.roll(hb, sh, 0), 0.0)
            hb = hb + ab * hsh
            ab = ab * ash
        hb = hb + ab * jnp.broadcast_to(carry, hb.shape)
        b_s[rows, :] = hb
        return hb[SUBLANES - 1:SUBLANES, :]

    car_ref[...] = lax.fori_loop(0, tc // SUBLANES, body, car_ref[...], unroll=2)
    o_ref[...] = _gelu_tanh(gate_ref[...]) * b_s[...]


def _rglru(p, conv_w, conv_b, w_x, b_x, w_a, b_a, lam, tc=512):
    t = p.shape[0]
    w = GROUP_W
    blk = jnp.arange(w) // (w // w_x.shape[0])
    same = blk[:, None] == blk[None, :]
    bdiag = lambda m: jnp.where(same, jnp.tile(m.reshape(w, -1), (1, m.shape[0])), 0.0).astype(BF16)
    vecs = jnp.concatenate([jnp.stack([conv_b, b_x, b_a, lam]), jnp.zeros((4, w), F32)], axis=0)
    const = lambda shape: pl.BlockSpec(shape, lambda i: (0,) * len(shape))
    return pl.pallas_call(
        functools.partial(_rglru_kernel, tc=tc),
        grid=(t // tc,),
        in_specs=[pl.BlockSpec((tc, w), lambda i: (i, 8)), pl.BlockSpec((tc, w), lambda i: (i, 9)),
                  const((RG_CONV, w)), const((8, w)), const((w, w)), const((w, w))],
        out_specs=pl.BlockSpec((tc, w), lambda i: (i, 0)),
        out_shape=jax.ShapeDtypeStruct((t, w), F32),
        scratch_shapes=[pltpu.VMEM((tc + SUBLANES, w), F32), pltpu.VMEM((tc, w), F32),
                        pltpu.VMEM((tc, w), F32), pltpu.VMEM((1, w), F32)],
        compiler_params=_cparams("arbitrary"),
        name="rglru",
    )(p, p, conv_w, vecs, bdiag(w_x), bdiag(w_a))


def kernel(x, ffn1_norm, ffn1_w1, ffn1_w3, ffn1_w2, mix_norm, w_in, s5_lambda_re, s5_lambda_im, s5_log_dt, s5_b_re, s5_b_im, s5_c_re, s5_c_im, s5_d, s5_glu_w, s5_glu_b, sb_q_gain, sb_k_gain, rw_mu, rw_w0, rw_w_a, rw_w_b, rw_a0, rw_a_a, rw_a_b, rw_g_a, rw_g_b, rw_k_k, rw_k_a, rw_r_k, rw_ln_g, rw_ln_b, rg_conv_w, rg_conv_b, rg_w_x, rg_b_x, rg_w_a, rg_b_a, rg_lambda, out_norm, w_out, ffn2_norm, ffn2_w1, ffn2_w3, ffn2_w2):
    bn, t, _ = x.shape
    assert bn == 1
    x = x.reshape(t, D_MODEL)
    bf = lambda z: z.astype(BF16)
    for l in range(ffn1_norm.shape[0]):
        x = _ffn(x, ffn1_norm[l], bf(ffn1_w1[l]), bf(ffn1_w3[l]), bf(ffn1_w2[l]))
        p = _inproj(x, mix_norm[l], bf(w_in[l]))
        ab, bb = _s5_prep(s5_lambda_re[l], s5_lambda_im[l], s5_log_dt[l], s5_b_re[l], s5_b_im[l])
        y_a = _s5(p, ab, bb, s5_c_re[l], s5_c_im[l], s5_d[l], s5_glu_w[l], s5_glu_b[l])
        y_b = _sb(p, sb_q_gain[l], sb_k_gain[l])
        y_c = _rwkv(p, rw_mu[l], rw_w0[l], rw_w_a[l], rw_w_b[l], rw_a0[l], rw_a_a[l], rw_a_b[l],
                    rw_g_a[l], rw_g_b[l], rw_k_k[l], rw_k_a[l], rw_r_k[l], rw_ln_g[l], rw_ln_b[l])
        y_d = _rglru(p, rg_conv_w[l], rg_conv_b[l], rg_w_x[l], rg_b_x[l], rg_w_a[l], rg_b_a[l],
                     rg_lambda[l])
        x = _outproj(x, (y_a, y_b, y_c, y_d), out_norm[l], bf(w_out[l]))
        x = _ffn(x, ffn2_norm[l], bf(ffn2_w1[l]), bf(ffn2_w3[l]), bf(ffn2_w2[l]))
    return x.reshape(bn, t, D_MODEL)
```

```python
import functools
import math

import jax
import jax.numpy as jnp
from jax import lax
from jax.experimental import pallas as pl
from jax.experimental.pallas import tpu as pltpu

F32 = jnp.float32
BF16 = jnp.bfloat16

D_MODEL = 2048
GROUP_W = 512
D_FF = 5632
D_IN = 10 * GROUP_W
S5_CH = 16
S5_GROUPS = 32
S5_STATE = 64
S5_LANES = S5_GROUPS * S5_STATE
SB_HEADS = 4
SB_HEAD_DIM = 128
RW_HEAD_DIM = 64
RW_GN_EPS = 64e-5
RG_CONV = 4
RG_C = 8.0
NORM_EPS = 1e-6
LOG2E = 1.4426950408889634

SUBLANES = 8
VMEM_LIMIT = 56 * 1024 * 1024

HI = lax.Precision.HIGHEST


def _cparams(*sem):
    return pltpu.CompilerParams(dimension_semantics=sem, vmem_limit_bytes=VMEM_LIMIT)


def _dot(a, b):
    return jnp.dot(a, b, preferred_element_type=F32)


def _dot_nt(a, b):
    return lax.dot_general(a, b, (((1,), (1,)), ((), ())), preferred_element_type=F32)


def _split_dot(x, m_bf16):
    hi = x.astype(BF16)
    lo = (x - hi.astype(F32)).astype(BF16)
    return _dot(hi, m_bf16) + _dot(lo, m_bf16)


def _rms(x, g):
    return x * lax.rsqrt(jnp.mean(x * x, axis=-1, keepdims=True) + NORM_EPS) * g


def _sigmoid(x):
    return 1.0 / (1.0 + jnp.exp(-x))


def _softplus(x):
    return jnp.maximum(x, 0.0) + jnp.log1p(jnp.exp(-jnp.abs(x)))


def _gelu_tanh(x):
    c = math.sqrt(2.0 / math.pi)
    return 0.5 * x * (1.0 + jnp.tanh(c * (x + 0.044715 * (x * x * x))))


def _ffn_kernel(x_ref, g_ref, w1_ref, w3_ref, w2_ref, o_ref, h_ref):
    @pl.when(pl.program_id(1) == 0)
    def _():
        x = x_ref[...]
        h_ref[...] = _rms(x, g_ref[...]).astype(BF16)
        o_ref[...] = x

    h = h_ref[...]
    a = _dot(h, w1_ref[...].astype(BF16))
    b = _dot(h, w3_ref[...].astype(BF16))
    act = (0.5 * (a * _sigmoid(a)) * b).astype(BF16)
    o_ref[...] += _dot(act, w2_ref[...].astype(BF16))


def _ffn(x, g, w1, w3, w2, l, tm=1024, tf=256):
    t = x.shape[0]
    return pl.pallas_call(
        _ffn_kernel,
        grid=(t // tm, D_FF // tf),
        in_specs=[
            pl.BlockSpec((tm, D_MODEL), lambda i, j: (i, 0), pipeline_mode=pl.Buffered(1)),
            pl.BlockSpec((1, D_MODEL), lambda i, j: (0, 0)),
            pl.BlockSpec((None, D_MODEL, tf), lambda i, j: (l, 0, j)),
            pl.BlockSpec((None, D_MODEL, tf), lambda i, j: (l, 0, j)),
            pl.BlockSpec((None, tf, D_MODEL), lambda i, j: (l, j, 0)),
        ],
        out_specs=pl.BlockSpec((tm, D_MODEL), lambda i, j: (i, 0)),
        out_shape=jax.ShapeDtypeStruct((t, D_MODEL), F32),
        scratch_shapes=[pltpu.VMEM((tm, D_MODEL), BF16)],
        compiler_params=_cparams("parallel", "arbitrary"),
        name="ffn",
    )(x, g.reshape(1, D_MODEL), w1, w3, w2)


def _inproj_kernel(x_ref, g_ref, w_ref, o_ref, h_ref):
    @pl.when(pl.program_id(1) == 0)
    def _():
        h_ref[...] = _rms(x_ref[...], g_ref[...]).astype(BF16)

    o_ref[...] = _dot(h_ref[...], w_ref[...].astype(BF16))


def _inproj(x, g, w, l, tm=1024, tn=1024):
    t = x.shape[0]
    return pl.pallas_call(
        _inproj_kernel,
        grid=(t // tm, D_IN // tn),
        in_specs=[
            pl.BlockSpec((tm, D_MODEL), lambda i, j: (i, 0), pipeline_mode=pl.Buffered(1)),
            pl.BlockSpec((1, D_MODEL), lambda i, j: (0, 0)),
            pl.BlockSpec((None, D_MODEL, tn), lambda i, j: (l, 0, j)),
        ],
        out_specs=pl.BlockSpec((tm, tn), lambda i, j: (i, j)),
        out_shape=jax.ShapeDtypeStruct((t, D_IN), F32),
        scratch_shapes=[pltpu.VMEM((tm, D_MODEL), BF16)],
        compiler_params=_cparams("parallel", "arbitrary"),
        name="inproj",
    )(x, g.reshape(1, D_MODEL), w)


def _outproj_kernel(x_ref, ya_ref, yb_ref, yc_ref, yd_ref, g_ref, w_ref, o_ref):
    acc = x_ref[...]
    for n, y_ref in enumerate((ya_ref, yb_ref, yc_ref, yd_ref)):
        yn = _rms(y_ref[...], g_ref[n:n + 1, :]).astype(BF16)
        acc = acc + _dot(yn, w_ref[n * GROUP_W:(n + 1) * GROUP_W, :])
    o_ref[...] = acc


def _outproj(x, ys, g, w, tm=512):
    t = x.shape[0]
    yspec = pl.BlockSpec((tm, GROUP_W), lambda i: (i, 0))
    return pl.pallas_call(
        _outproj_kernel,
        grid=(t // tm,),
        in_specs=[
            pl.BlockSpec((tm, D_MODEL), lambda i: (i, 0)),
            yspec, yspec, yspec, yspec,
            pl.BlockSpec((4, GROUP_W), lambda i: (0, 0)),
            pl.BlockSpec((D_MODEL, D_MODEL), lambda i: (0, 0)),
        ],
        out_specs=pl.BlockSpec((tm, D_MODEL), lambda i: (i, 0)),
        out_shape=jax.ShapeDtypeStruct((t, D_MODEL), F32),
        compiler_params=_cparams("parallel"),
        name="outproj",
    )(x, *ys, g.reshape(4, GROUP_W), w)


def _s5_prep_kernel(lr_ref, li_ref, ldt_ref, br_ref, bi_ref, ab_ref, bb_ref):
    lr, li = lr_ref[...], li_ref[...]
    dt = jnp.exp(ldt_ref[...])
    mag = jnp.exp(lr * dt)
    ab_re, ab_im = mag * jnp.cos(li * dt), mag * jnp.sin(li * dt)
    den = lr * lr + li * li
    g_re = ((ab_re - 1.0) * lr + ab_im * li) / den
    g_im = (ab_im * lr - (ab_re - 1.0) * li) / den
    br, bi = br_ref[...], bi_ref[...]
    ab_ref[0:1, :] = ab_re
    ab_ref[1:2, :] = ab_im
    bb_ref[0] = g_re * br - g_im * bi
    bb_ref[1] = g_re * bi + g_im * br


def _s5_prep(lam_re, lam_im, log_dt, b_re, b_im):
    row = lambda z: z.reshape(1, S5_LANES)
    ldt = jnp.broadcast_to(log_dt[:, None], (S5_GROUPS, S5_STATE))
    tr = lambda b: b.transpose(2, 0, 1).reshape(S5_CH, S5_LANES)
    return pl.pallas_call(
        _s5_prep_kernel,
        out_shape=(jax.ShapeDtypeStruct((2, S5_LANES), F32),
                   jax.ShapeDtypeStruct((2, S5_CH, S5_LANES), F32)),
        name="s5_prep",
    )(row(lam_re), row(lam_im), row(ldt), tr(b_re), tr(b_im))


def _cmul(ar, ai, br, bi):
    return ar * br - ai * bi, ar * bi + ai * br


S5_SPLIT = 4


def _s5_kernel(u_ref, ab_ref, bb_ref, cc_ref, d_ref, gw_ref, gb_ref, o_ref,
               s_ref, tab_ref, car_ref, bmat_ref, cmat_ref, *, tc, cw):
    n = S5_LANES
    nb = n // S5_SPLIT
    cb = GROUP_W // S5_SPLIT

    @pl.when(pl.program_id(0) == 0)
    def _():
        ar = jnp.broadcast_to(ab_ref[0:1, :], (SUBLANES, n))
        ai = jnp.broadcast_to(ab_ref[1:2, :], (SUBLANES, n))
        row = lax.broadcasted_iota(jnp.int32, (SUBLANES, n), 0)
        a2r, a2i = _cmul(ar, ai, ar, ai)
        a4r, a4i = _cmul(a2r, a2i, a2r, a2i)
        for k, (pr, pi, sh) in enumerate(((ar, ai, 1), (a2r, a2i, 2), (a4r, a4i, 4))):
            tab_ref[2 * k] = jnp.where(row >= sh, pr, 0.0)
            tab_ref[2 * k + 1] = jnp.where(row >= sh, pi, 0.0)
        pwr, pwi = ar, ai
        for bit, (qr, qi) in ((1, (ar, ai)), (2, (a2r, a2i)), (4, (a4r, a4i))):
            nr, ni = _cmul(pwr, pwi, qr, qi)
            sel = (row & bit) != 0
            pwr, pwi = jnp.where(sel, nr, pwr), jnp.where(sel, ni, pwi)
        tab_ref[6] = pwr
        tab_ref[7] = pwi
        car_ref[...] = jnp.zeros_like(car_ref)
        lane_grp = lax.broadcasted_iota(jnp.int32, (S5_CH, nb), 1) // S5_STATE
        for c in range(S5_SPLIT):
            st = slice(c * nb, (c + 1) * nb)
            for ref, src, sign in ((bmat_ref, bb_ref, 1.0), (cmat_ref, cc_ref, -1.0)):
                re, im = src[0, :, st], src[1, :, st] * sign
                ref[c] = jnp.concatenate(
                    [jnp.concatenate([jnp.where(lane_grp == g, re, 0.0), jnp.where(lane_grp == g, im, 0.0)], axis=1)
                     for g in range(cb // S5_CH)], axis=0).astype(BF16)

    u = u_ref[...]
    ub = u.astype(BF16)
    for c in range(S5_SPLIT):
        s_ref[:, 2 * c * nb:2 * (c + 1) * nb] = _dot(ub[:, c * cb:(c + 1) * cb], bmat_ref[c])

    cols = [(slice(c * nb + k * cw, c * nb + (k + 1) * cw),
             pl.ds(2 * c * nb + k * cw, cw), pl.ds(2 * c * nb + nb + k * cw, cw))
            for c in range(S5_SPLIT) for k in range(nb // cw)]

    def body(i, carries):
        rows = pl.ds(pl.multiple_of(i * SUBLANES, SUBLANES), SUBLANES)
        out = []
        for (st, cre, cim), (cr, ci) in zip(cols, carries):
            xr, xi = s_ref[rows, cre], s_ref[rows, cim]
            for m, sh in enumerate((1, 2, 4)):
                dr, di = _cmul(tab_ref[2 * m, :, st], tab_ref[2 * m + 1, :, st],
                               pltpu.roll(xr, sh, 0), pltpu.roll(xi, sh, 0))
                xr, xi = xr + dr, xi + di
            dr, di = _cmul(tab_ref[6, :, st], tab_ref[7, :, st],
                           jnp.broadcast_to(cr, xr.shape), jnp.broadcast_to(ci, xi.shape))
            xr, xi = xr + dr, xi + di
            s_ref[rows, cre] = xr
            s_ref[rows, cim] = xi
            out.append((xr[SUBLANES - 1:SUBLANES, :], xi[SUBLANES - 1:SUBLANES, :]))
        return tuple(out)

    carries = lax.fori_loop(0, tc // SUBLANES, body,
                            tuple((car_ref[0:1, st], car_ref[1:2, st]) for st, _, _ in cols))
    for (st, _, _), (cr, ci) in zip(cols, carries):
        car_ref[0:1, st] = cr
        car_ref[1:2, st] = ci

    y = jnp.concatenate([_dot_nt(s_ref[:, 2 * c * nb:2 * (c + 1) * nb].astype(BF16), cmat_ref[c])
                         for c in range(S5_SPLIT)], axis=1) + d_ref[...] * u
    v = _gelu_tanh(y)
    o_ref[...] = v * _sigmoid(_dot(v.astype(BF16), gw_ref[...]) + gb_ref[...])


def _s5(p, ab, bb, c_re, c_im, d, glu_w, glu_b, tc=512, cw=256):
    t = p.shape[0]
    n = S5_LANES
    cc = jnp.stack([z.transpose(1, 0, 2).reshape(S5_CH, n) for z in (c_re, c_im)])
    const = lambda shape: pl.BlockSpec(shape, lambda i: (0,) * len(shape))
    blk = (S5_SPLIT, GROUP_W // S5_SPLIT, 2 * n // S5_SPLIT)
    return pl.pallas_call(
        functools.partial(_s5_kernel, tc=tc, cw=cw),
        grid=(t // tc,),
        in_specs=[
            pl.BlockSpec((tc, GROUP_W), lambda i: (i, 0)),
            const((2, n)), const((2, S5_CH, n)), const((2, S5_CH, n)),
            const((1, GROUP_W)), const((GROUP_W, GROUP_W)), const((1, GROUP_W)),
        ],
        out_specs=pl.BlockSpec((tc, GROUP_W), lambda i: (i, 0)),
        out_shape=jax.ShapeDtypeStruct((t, GROUP_W), F32),
        scratch_shapes=[pltpu.VMEM((tc, 2 * n), F32),
                        pltpu.VMEM((8, SUBLANES, n), F32),
                        pltpu.VMEM((2, n), F32),
                        pltpu.VMEM(blk, BF16), pltpu.VMEM(blk, BF16)],
        compiler_params=_cparams("arbitrary"),
        name="s5",
    )(p, ab, bb, cc, d.reshape(1, GROUP_W), glu_w.astype(BF16), glu_b.reshape(1, GROUP_W))


def _sb_kernel(q_ref, k_ref, v_ref, qg_ref, kg_ref, tri_ref, o_ref, kn_ref, vb_ref, qn_ref, acc_ref,
               z_ref, d_ref, after_ref, w_ref, *, tq, tk, ts):
    qi = pl.program_id(1)
    nd = tq // tk
    nfull = qi * nd

    @pl.when(qi == 0)
    def _():
        kn_ref[...] = _rms(k_ref[...], kg_ref[...]).astype(BF16)
        vb_ref[...] = v_ref[...].astype(BF16)

    qn_ref[...] = (_rms(q_ref[...], qg_ref[...]) * (SB_HEAD_DIM ** -0.5)).astype(BF16)
    tri = tri_ref[...]
    diff = (lax.broadcasted_iota(jnp.int32, (ts, tk), 1) - lax.broadcasted_iota(jnp.int32, (ts, tk), 0))
    subs = tuple(range(0, tq, ts))
    krows = lambda tile: pl.ds(pl.multiple_of(tile * tk, tk), tk)

    def scores(tile):
        return _dot_nt(qn_ref[...], kn_ref[krows(tile), :])

    ntiles = nd + nfull
    hidden = -1e30

    def step(it, par, state, ds, do_q=True, do_n=True, do_w=True, do_p=True):
        cs, rs = state
        if do_q:
            z_next = scores(jnp.maximum(ntiles - 2 - it, 0))
        if do_p:
            pv = _dot(w_ref[par], vb_ref[krows(ntiles + 1 - it), :])
        if do_n:
            dvs, afters, new_rs = [], [], []
            for r0, d in zip(subs, ds):
                if d is False:
                    dvs.append(jnp.full((ts, tk), hidden, F32))
                    afters.append(jnp.zeros((ts, tk), F32))
                    new_rs.append(jnp.zeros((ts, 1), F32))
                    continue
                z = z_ref[par, r0:r0 + ts, :]
                nl = jnp.maximum(z, 0.0) + jnp.log(1.0 + jnp.exp2(jnp.abs(z) * (-LOG2E)))
                dv = z - nl
                if d is not None:
                    nl, dv = jnp.where(diff < d, nl, 0.0), jnp.where(diff < d, dv, hidden)
                dvs.append(dv)
                afters.append(_dot(nl.astype(BF16), tri))
                new_rs.append(jnp.sum(nl, axis=-1, keepdims=True))
            d_ref[par] = jnp.concatenate(dvs, axis=0)
        if do_w:
            ws = [jnp.exp(d_ref[1 - par, r0:r0 + ts, :] - after_ref[1 - par, r0:r0 + ts, :] - c).astype(BF16)
                  for r0, c in zip(subs, cs)]
            w_ref[1 - par] = jnp.concatenate(ws, axis=0)
            cs = tuple(c + r for c, r in zip(cs, rs))
        if do_n:
            after_ref[par] = jnp.concatenate(afters, axis=0)
            rs = tuple(new_rs)
        if do_q:
            z_ref[1 - par] = z_next
        if do_p:
            acc_ref[...] += pv
        return cs, rs

    z_ref[0] = scores(ntiles - 1)
    acc_ref[...] = jnp.zeros_like(acc_ref)
    zeros = tuple(jnp.zeros((ts, 1), F32) for _ in subs)
    state = (zeros, zeros)
    for it in range(nd):
        kd = (nd - 1 - it) * tk
        ds = [False if r0 - kd <= -(ts - 1) else (None if r0 - kd >= tk else r0 - kd) for r0 in subs]
        state = step(it, it % 2, state, ds, do_w=it >= 1, do_p=it >= 2)

    def two_trips(n, st):
        for par in range(2):
            st = step(nd + 2 * n + par, par, st, [None] * len(subs))
        return st

    state = lax.fori_loop(0, nfull // 2, two_trips, state)
    step(ntiles, 0, state, None, do_q=False, do_n=False)
    o_ref[...] = acc_ref[...] + _dot(w_ref[1], vb_ref[krows(0), :])


def _sb(p, q_gain, k_gain, tq=512, tk=256, ts=128):
    t = p.shape[0]
    tri = jnp.tril(jnp.ones((tk, tk), F32), -1).astype(BF16)
    return pl.pallas_call(
        functools.partial(_sb_kernel, tq=tq, tk=tk, ts=ts),
        grid=(SB_HEADS, t // tq),
        in_specs=[
            pl.BlockSpec((tq, SB_HEAD_DIM), lambda h, i: (i, 4 + h)),
            pl.BlockSpec((t, SB_HEAD_DIM), lambda h, i: (0, 8 + h)),
            pl.BlockSpec((t, SB_HEAD_DIM), lambda h, i: (0, 12 + h)),
            pl.BlockSpec((1, SB_HEAD_DIM), lambda h, i: (0, 0)),
            pl.BlockSpec((1, SB_HEAD_DIM), lambda h, i: (0, 0)),
            pl.BlockSpec((tk, tk), lambda h, i: (0, 0)),
        ],
        out_specs=pl.BlockSpec((tq, SB_HEAD_DIM), lambda h, i: (i, h)),
        out_shape=jax.ShapeDtypeStruct((t, GROUP_W), F32),
        scratch_shapes=[pltpu.VMEM((t, SB_HEAD_DIM), BF16), pltpu.VMEM((t, SB_HEAD_DIM), BF16),
                        pltpu.VMEM((tq, SB_HEAD_DIM), BF16), pltpu.VMEM((tq, SB_HEAD_DIM), F32),
                        pltpu.VMEM((2, tq, tk), F32), pltpu.VMEM((2, tq, tk), F32),
                        pltpu.VMEM((2, tq, tk), F32), pltpu.VMEM((2, tq, tk), BF16)],
        compiler_params=_cparams("arbitrary", "arbitrary"),
        name="stickbreak",
    )(p, p, p, q_gain.reshape(1, SB_HEAD_DIM), k_gain.reshape(1, SB_HEAD_DIM), tri)


RW_CHUNK = 64
(_MU_R, _MU_K, _MU_V, _MU_W, _MU_A, _MU_G, _W0, _A0, _KK, _KA, _RK, _LNG, _LNB) = range(13)


_BNN = (((2,), (1,)), ((0,), (0,)))
_BNT = (((2,), (2,)), ((0,), (0,)))


def _bmm(a, b, dims=_BNN):
    return lax.dot_general(a.astype(BF16), b.astype(BF16), dims, preferred_element_type=F32)


def _split_dot_left(m_bf16, x):
    hi = x.astype(BF16)
    lo = (x - hi.astype(F32)).astype(BF16)
    return _dot(m_bf16, hi) + _dot(m_bf16, lo)


def _rwkv_kernel(r_ref, k_ref, v_ref, m_ref, vec_ref, wa_ref, wb_ref, aa_ref, ab_ref, ga_ref, gb_ref,
                 ones_ref, tril_ref, blk_ref, o_ref, zs_ref, st_ref, *, tc):
    first = pl.program_id(0) == 0
    w = GROUP_W
    vec = lambda i: vec_ref[i:i + 1, :]

    @pl.when(first)
    def _():
        zs_ref[...] = jnp.zeros_like(zs_ref)
        st_ref[...] = jnp.zeros_like(st_ref)

    cur, prev = [], []
    for n, ref in enumerate((r_ref, k_ref, v_ref, m_ref)):
        zs_ref[n, SUBLANES - 1:SUBLANES, :] = zs_ref[n, SUBLANES + tc - 1:SUBLANES + tc, :]
        z = ref[...]
        zs_ref[n, SUBLANES:SUBLANES + tc, :] = z
        cur.append(z)
        prev.append(zs_ref[n, SUBLANES - 1:SUBLANES - 1 + tc, :])
    lerp = lambda n, mu: cur[n] + (prev[n] - cur[n]) * vec(mu)
    r, k, v = lerp(0, _MU_R), lerp(1, _MU_K), lerp(2, _MU_V)
    xw, xa, xg = lerp(3, _MU_W), lerp(3, _MU_A), lerp(3, _MU_G)

    lora = lambda x, a_ref, b_ref, f: _dot(f(_dot(x.astype(BF16), a_ref[...])).astype(BF16), b_ref[...])
    w_log = -_softplus(-(vec(_W0) + lora(xw, wa_ref, wb_ref, jnp.tanh))) - 0.5
    lw = -jnp.exp(w_log)
    a = _sigmoid(vec(_A0) + lora(xa, aa_ref, ab_ref, lambda z: z))
    g = lora(xg, ga_ref, gb_ref, _sigmoid)
    ones = ones_ref[...]
    kk = k * vec(_KK)
    kk = kk * lax.rsqrt(_split_dot(kk * kk, ones) + 1e-12)
    k = k * (1.0 + (a - 1.0) * vec(_KA))
    bonus = _split_dot(r * k * vec(_RK), ones) * v
    a_in, b_in = -kk, kk * a

    L = RW_CHUNK
    nch, nhp = tc // L, w // 128
    cum = _split_dot_left(tril_ref[...], lw)
    cum_l = _split_dot_left(blk_ref[...], lw)
    g_inv, g_end = jnp.exp(-cum), jnp.exp(cum_l - cum)
    head0 = lax.broadcasted_iota(jnp.int32, (L, 128), 1) < RW_HEAD_DIM

    def stacked(x, split_heads=True):
        out = []
        for c in range(nch):
            for hp in range(nhp):
                z = x[c * L:(c + 1) * L, hp * 128:(hp + 1) * 128]
                out.append(jnp.concatenate([jnp.where(head0, z, 0.0), jnp.where(head0, 0.0, z)], axis=0)
                           if split_heads else jnp.concatenate([z, z], axis=0))
        return jnp.stack(out)

    xa_, xr_ = stacked(a_in * jnp.exp(cum - lw)).astype(BF16), stacked(r * jnp.exp(cum)).astype(BF16)
    xb_, xk_, xv_ = stacked(b_in * g_inv).astype(BF16), stacked(k * g_inv).astype(BF16), stacked(v).astype(BF16)
    xbl_t = jnp.swapaxes(stacked(b_in * g_end), 1, 2).astype(BF16)
    xkl_t = jnp.swapaxes(stacked(k * g_end), 1, 2).astype(BF16)
    g_col = jnp.swapaxes(stacked(jnp.exp(cum_l), split_heads=False), 1, 2)

    ri = lax.broadcasted_iota(jnp.int32, (2 * L, 2 * L), 0)
    ci = lax.broadcasted_iota(jnp.int32, (2 * L, 2 * L), 1)
    strict, incl = ri > ci, ri >= ci
    eye = (ri == ci).astype(F32)
    pm = _bmm(jnp.concatenate([xa_, xr_], axis=1), jnp.concatenate([xb_, xk_], axis=1), _BNT)
    n_ab = jnp.where(strict, pm[:, :2 * L, :2 * L], 0.0)
    a_ak = jnp.where(strict, pm[:, :2 * L, 2 * L:], 0.0)
    a_rb = jnp.where(incl, pm[:, 2 * L:, :2 * L], 0.0).astype(BF16)
    a_rk = jnp.where(incl, pm[:, 2 * L:, 2 * L:], 0.0)
    tinv, npow = eye + n_ab, n_ab
    for _ in range(int(math.log2(L)) - 1):
        npow = _bmm(npow, npow)
        tinv = tinv + _bmm(npow, tinv)
    wu = _bmm(tinv, jnp.concatenate([xa_, _bmm(a_ak, xv_).astype(BF16)], axis=2))
    wx = jnp.concatenate([wu[:, :, :128].astype(BF16), xr_], axis=1)
    u0 = wu[:, :, 128:]
    y0 = _bmm(a_rk, xv_)
    s0 = _bmm(xkl_t, xv_)

    st = st_ref[...]
    ys = []
    for c in range(nch):
        sl = slice(c * nhp, (c + 1) * nhp)
        gs = _bmm(wx[sl], st)
        u = (gs[:, :2 * L] + u0[sl]).astype(BF16)
        yst = gs[:, 2 * L:] + _bmm(a_rb[sl], u) + y0[sl]
        st = g_col[sl] * st + _bmm(xbl_t[sl], u) + s0[sl]
        ys.append(jnp.concatenate([yst[hp, :L] + yst[hp, L:] for hp in range(nhp)], axis=1))
    st_ref[...] = st

    y = jnp.concatenate(ys, axis=0)
    inv_n = 1.0 / RW_HEAD_DIM
    mean = _split_dot(y, ones) * inv_n
    yc = y - mean
    var = _split_dot(yc * yc, ones) * inv_n
    y = yc * lax.rsqrt(var + RW_GN_EPS) * vec(_LNG) + vec(_LNB)
    o_ref[...] = (y + bonus) * g


def _rwkv(p, mu, w0, w_a, w_b, a0, a_a, a_b, g_a, g_b, k_k, k_a, r_k, ln_g, ln_b, tc=256):
    t = p.shape[0]
    w = GROUP_W
    vecs = jnp.concatenate([mu, jnp.stack([w0, a0, k_k, k_a, r_k.reshape(w), ln_g, ln_b]),
                            jnp.zeros((3, w), F32)], axis=0)
    head = jnp.arange(w) // RW_HEAD_DIM
    ones = (head[:, None] == head[None, :]).astype(BF16)
    chunk = jnp.arange(tc) // RW_CHUNK
    blk = chunk[:, None] == chunk[None, :]
    tril = jnp.logical_and(blk, jnp.arange(tc)[:, None] >= jnp.arange(tc)[None, :])
    const = lambda shape: pl.BlockSpec(shape, lambda i: (0,) * len(shape))
    col = lambda c: pl.BlockSpec((tc, w), lambda i: (i, c))
    bf = lambda z: z.astype(BF16)
    return pl.pallas_call(
        functools.partial(_rwkv_kernel, tc=tc),
        grid=(t // tc,),
        in_specs=[col(4), col(5), col(6), col(7), const((16, w)),
                  const(w_a.shape), const(w_b.shape), const(a_a.shape), const(a_b.shape),
                  const(g_a.shape), const(g_b.shape), const((w, w)), const((tc, tc)), const((tc, tc))],
        out_specs=pl.BlockSpec((tc, w), lambda i: (i, 0)),
        out_shape=jax.ShapeDtypeStruct((t, w), F32),
        scratch_shapes=[pltpu.VMEM((4, tc + SUBLANES, w), F32), pltpu.VMEM((w // 128, 128, 128), F32)],
        compiler_params=_cparams("arbitrary"),
        name="rwkv7",
    )(p, p, p, p, vecs, bf(w_a), bf(w_b), bf(a_a), bf(a_b), bf(g_a), bf(g_b), ones, bf(tril), bf(blk))


def _rglru_kernel(gate_ref, x_ref, cw_ref, vec_ref, wx_ref, wa_ref, o_ref,
                  xs_ref, a_s, b_s, car_ref, *, tc):
    first = pl.program_id(0) == 0
    halo = SUBLANES

    @pl.when(first)
    def _():
        xs_ref[...] = jnp.zeros_like(xs_ref)
        car_ref[...] = jnp.zeros_like(car_ref)

    xs_ref[0:halo, :] = xs_ref[tc:tc + halo, :]
    xs_ref[halo:halo + tc, :] = x_ref[...]
    xc = vec_ref[0:1, :]
    for j in range(RG_CONV):
        off = halo - (RG_CONV - 1) + j
        xc = xc + xs_ref[off:off + tc, :] * cw_ref[j:j + 1, :]
    xcb = xc.astype(BF16)
    gate_x = _sigmoid(_dot(xcb, wx_ref[...]) + vec_ref[1:2, :])
    gate_a = _sigmoid(_dot(xcb, wa_ref[...]) + vec_ref[2:3, :])
    lam = vec_ref[3:4, :]
    log_a = RG_C * gate_a * (-_softplus(-lam))
    a = jnp.exp(log_a)
    mult = jnp.sqrt(1.0 - a * a)
    t_idx = lax.broadcasted_iota(jnp.int32, (tc, GROUP_W), 0)
    mult = jnp.where(jnp.logical_and(first, t_idx == 0), 1.0, mult)
    a_s[...] = a
    b_s[...] = xc * gate_x * mult

    row = lax.broadcasted_iota(jnp.int32, (SUBLANES, GROUP_W), 0)

    def body(i, carry):
        rows = pl.ds(pl.multiple_of(i * SUBLANES, SUBLANES), SUBLANES)
        ab, hb = a_s[rows, :], b_s[rows, :]
        for sh in (1, 2, 4):
            ash = jnp.where(row >= sh, pltpu.roll(ab, sh, 0), 1.0)
            hsh = jnp.where(row >= sh, pltpu.roll(hb, sh, 0), 0.0)
            hb = hb + ab * hsh
            ab = ab * ash
        hb = hb + ab * jnp.broadcast_to(carry, hb.shape)
        b_s[rows, :] = hb
        return hb[SUBLANES - 1:SUBLANES, :]

    car_ref[...] = lax.fori_loop(0, tc // SUBLANES, body, car_ref[...], unroll=2)
    o_ref[...] = _gelu_tanh(gate_ref[...]) * b_s[...]


def _rglru(p, conv_w, conv_b, w_x, b_x, w_a, b_a, lam, tc=512):
    t = p.shape[0]
    w = GROUP_W
    blk = jnp.arange(w) // (w // w_x.shape[0])
    same = blk[:, None] == blk[None, :]
    bdiag = lambda m: jnp.where(same, jnp.tile(m.reshape(w, -1), (1, m.shape[0])), 0.0).astype(BF16)
    vecs = jnp.concatenate([jnp.stack([conv_b, b_x, b_a, lam]), jnp.zeros((4, w), F32)], axis=0)
    const = lambda shape: pl.BlockSpec(shape, lambda i: (0,) * len(shape))
    return pl.pallas_call(
        functools.partial(_rglru_kernel, tc=tc),
        grid=(t // tc,),
        in_specs=[pl.BlockSpec((tc, w), lambda i: (i, 8)), pl.BlockSpec((tc, w), lambda i: (i, 9)),
                  const((RG_CONV, w)), const((8, w)), const((w, w)), const((w, w))],
        out_specs=pl.BlockSpec((tc, w), lambda i: (i, 0)),
        out_shape=jax.ShapeDtypeStruct((t, w), F32),
        scratch_shapes=[pltpu.VMEM((tc + SUBLANES, w), F32), pltpu.VMEM((tc, w), F32),
                        pltpu.VMEM((tc, w), F32), pltpu.VMEM((1, w), F32)],
        compiler_params=_cparams("arbitrary"),
        name="rglru",
    )(p, p, conv_w, vecs, bdiag(w_x), bdiag(w_a))


def kernel(x, ffn1_norm, ffn1_w1, ffn1_w3, ffn1_w2, mix_norm, w_in, s5_lambda_re, s5_lambda_im, s5_log_dt, s5_b_re, s5_b_im, s5_c_re, s5_c_im, s5_d, s5_glu_w, s5_glu_b, sb_q_gain, sb_k_gain, rw_mu, rw_w0, rw_w_a, rw_w_b, rw_a0, rw_a_a, rw_a_b, rw_g_a, rw_g_b, rw_k_k, rw_k_a, rw_r_k, rw_ln_g, rw_ln_b, rg_conv_w, rg_conv_b, rg_w_x, rg_b_x, rg_w_a, rg_b_a, rg_lambda, out_norm, w_out, ffn2_norm, ffn2_w1, ffn2_w3, ffn2_w2):
    bn, t, _ = x.shape
    assert bn == 1
    x = x.reshape(t, D_MODEL)
    bf = lambda z: z.astype(BF16)
    for l in range(ffn1_norm.shape[0]):
        x = _ffn(x, ffn1_norm[l], ffn1_w1, ffn1_w3, ffn1_w2, l)
        p = _inproj(x, mix_norm[l], w_in, l)
        ab, bb = _s5_prep(s5_lambda_re[l], s5_lambda_im[l], s5_log_dt[l], s5_b_re[l], s5_b_im[l])
        y_a = _s5(p, ab, bb, s5_c_re[l], s5_c_im[l], s5_d[l], s5_glu_w[l], s5_glu_b[l])
        y_b = _sb(p, sb_q_gain[l], sb_k_gain[l])
        y_c = _rwkv(p, rw_mu[l], rw_w0[l], rw_w_a[l], rw_w_b[l], rw_a0[l], rw_a_a[l], rw_a_b[l],
                    rw_g_a[l], rw_g_b[l], rw_k_k[l], rw_k_a[l], rw_r_k[l], rw_ln_g[l], rw_ln_b[l])
        y_d = _rglru(p, rg_conv_w[l], rg_conv_b[l], rg_w_x[l], rg_b_x[l], rg_w_a[l], rg_b_a[l],
                     rg_lambda[l])
        x = _outproj(x, (y_a, y_b, y_c, y_d), out_norm[l], bf(w_out[l]))
        x = _ffn(x, ffn2_norm[l], ffn2_w1, ffn2_w3, ffn2_w2, l)
    return x.reshape(bn, t, D_MODEL)
```

```python
import functools
import math

import jax
import jax.numpy as jnp
from jax import lax
from jax.experimental import pallas as pl
from jax.experimental.pallas import tpu as pltpu

F32 = jnp.float32
BF16 = jnp.bfloat16

D_MODEL = 2048
GROUP_W = 512
D_FF = 5632
D_IN = 10 * GROUP_W
S5_CH = 16
S5_GROUPS = 32
S5_STATE = 64
S5_LANES = S5_GROUPS * S5_STATE
SB_HEADS = 4
SB_HEAD_DIM = 128
RW_HEAD_DIM = 64
RW_GN_EPS = 64e-5
RG_CONV = 4
RG_C = 8.0
NORM_EPS = 1e-6
LOG2E = 1.4426950408889634

SUBLANES = 8
VMEM_LIMIT = 56 * 1024 * 1024
FFN_VMEM_LIMIT = 60 * 1024 * 1024

HI = lax.Precision.HIGHEST


def _cparams(*sem):
    return pltpu.CompilerParams(dimension_semantics=sem, vmem_limit_bytes=VMEM_LIMIT)


def _dot(a, b):
    return jnp.dot(a, b, preferred_element_type=F32)


def _dot_nt(a, b):
    return lax.dot_general(a, b, (((1,), (1,)), ((), ())), preferred_element_type=F32)


def _split_dot(x, m_bf16):
    hi = x.astype(BF16)
    lo = (x - hi.astype(F32)).astype(BF16)
    return _dot(hi, m_bf16) + _dot(lo, m_bf16)


def _rms(x, g):
    return x * lax.rsqrt(jnp.mean(x * x, axis=-1, keepdims=True) + NORM_EPS) * g


def _sigmoid(x):
    return 1.0 / (1.0 + jnp.exp(-x))


def _softplus(x):
    return jnp.maximum(x, 0.0) + jnp.log1p(jnp.exp(-jnp.abs(x)))


def _gelu_tanh(x):
    c = math.sqrt(2.0 / math.pi)
    return 0.5 * x * (1.0 + jnp.tanh(c * (x + 0.044715 * (x * x * x))))


def _ffn_kernel(x_hbm, g_ref, w1_ref, w3_ref, w2_ref, o_hbm, acc_ref, h_ref, ld_sem, st_sem, *, tm):
    i, j = pl.program_id(0), pl.program_id(1)
    ni, nj = pl.num_programs(0), pl.num_programs(1)
    slot = i % 2
    rows = lambda tile: pl.ds(pl.multiple_of(tile * tm, tm), tm)
    load = lambda tile, s: pltpu.make_async_copy(x_hbm.at[rows(tile), :], acc_ref.at[s], ld_sem.at[s])
    store = lambda tile, s: pltpu.make_async_copy(acc_ref.at[s], o_hbm.at[rows(tile), :], st_sem.at[s])

    @pl.when(jnp.logical_and(i == 0, j == 0))
    def _():
        load(0, 0).start()

    @pl.when(j == 0)
    def _():
        load(i, slot).wait()
        h_ref[...] = _rms(acc_ref[slot], g_ref[...]).astype(BF16)

    @pl.when(j == 1)
    def _():
        @pl.when(i >= 1)
        def _():
            store(i - 1, 1 - slot).wait()

        @pl.when(i + 1 < ni)
        def _():
            load(i + 1, 1 - slot).start()

    h = h_ref[...]
    a = _dot(h, w1_ref[...].astype(BF16))
    b = _dot(h, w3_ref[...].astype(BF16))
    act = (0.5 * (a * _sigmoid(a)) * b).astype(BF16)
    acc_ref[slot] += _dot(act, w2_ref[...].astype(BF16))

    @pl.when(j == nj - 1)
    def _():
        store(i, slot).start()

        @pl.when(i == ni - 1)
        def _():
            store(i, slot).wait()


def _ffn(x, g, w1, w3, w2, l, tm=1024, tf=512):
    t = x.shape[0]
    assert D_FF // tf >= 2 and t // tm >= 2
    return pl.pallas_call(
        functools.partial(_ffn_kernel, tm=tm),
        grid=(t // tm, D_FF // tf),
        in_specs=[
            pl.BlockSpec(memory_space=pl.ANY),
            pl.BlockSpec((1, D_MODEL), lambda i, j: (0, 0)),
            pl.BlockSpec((None, D_MODEL, tf), lambda i, j: (l, 0, j)),
            pl.BlockSpec((None, D_MODEL, tf), lambda i, j: (l, 0, j)),
            pl.BlockSpec((None, tf, D_MODEL), lambda i, j: (l, j, 0)),
        ],
        out_specs=pl.BlockSpec(memory_space=pl.ANY),
        out_shape=jax.ShapeDtypeStruct((t, D_MODEL), F32),
        scratch_shapes=[pltpu.VMEM((2, tm, D_MODEL), F32), pltpu.VMEM((tm, D_MODEL), BF16),
                        pltpu.SemaphoreType.DMA((2,)), pltpu.SemaphoreType.DMA((2,))],
        compiler_params=pltpu.CompilerParams(dimension_semantics=("arbitrary", "arbitrary"),
                                             vmem_limit_bytes=FFN_VMEM_LIMIT),
        name="ffn",
    )(x, g.reshape(1, D_MODEL), w1, w3, w2)


def _inproj_kernel(x_ref, g_ref, w_ref, o_ref, h_ref):
    @pl.when(pl.program_id(1) == 0)
    def _():
        h_ref[...] = _rms(x_ref[...], g_ref[...]).astype(BF16)

    o_ref[...] = _dot(h_ref[...], w_ref[...].astype(BF16))


def _inproj(x, g, w, l, tm=2048, tn=512):
    t = x.shape[0]
    return pl.pallas_call(
        _inproj_kernel,
        grid=(t // tm, D_IN // tn),
        in_specs=[
            pl.BlockSpec((tm, D_MODEL), lambda i, j: (i, 0), pipeline_mode=pl.Buffered(1)),
            pl.BlockSpec((1, D_MODEL), lambda i, j: (0, 0)),
            pl.BlockSpec((None, D_MODEL, tn), lambda i, j: (l, 0, j)),
        ],
        out_specs=pl.BlockSpec((tm, tn), lambda i, j: (i, j)),
        out_shape=jax.ShapeDtypeStruct((t, D_IN), F32),
        scratch_shapes=[pltpu.VMEM((tm, D_MODEL), BF16)],
        compiler_params=_cparams("parallel", "arbitrary"),
        name="inproj",
    )(x, g.reshape(1, D_MODEL), w)


def _outproj_kernel(x_ref, ya_ref, yb_ref, yc_ref, yd_ref, g_ref, w_ref, o_ref):
    acc = x_ref[...]
    for n, y_ref in enumerate((ya_ref, yb_ref, yc_ref, yd_ref)):
        yn = _rms(y_ref[...], g_ref[n:n + 1, :]).astype(BF16)
        acc = acc + _dot(yn, w_ref[n * GROUP_W:(n + 1) * GROUP_W, :])
    o_ref[...] = acc


def _outproj(x, ys, g, w, tm=512):
    t = x.shape[0]
    yspec = pl.BlockSpec((tm, GROUP_W), lambda i: (i, 0))
    return pl.pallas_call(
        _outproj_kernel,
        grid=(t // tm,),
        in_specs=[
            pl.BlockSpec((tm, D_MODEL), lambda i: (i, 0)),
            yspec, yspec, yspec, yspec,
            pl.BlockSpec((4, GROUP_W), lambda i: (0, 0)),
            pl.BlockSpec((D_MODEL, D_MODEL), lambda i: (0, 0)),
        ],
        out_specs=pl.BlockSpec((tm, D_MODEL), lambda i: (i, 0)),
        out_shape=jax.ShapeDtypeStruct((t, D_MODEL), F32),
        compiler_params=_cparams("parallel"),
        name="outproj",
    )(x, *ys, g.reshape(4, GROUP_W), w)


def _s5_prep_kernel(lr_ref, li_ref, ldt_ref, br_ref, bi_ref, ab_ref, bb_ref):
    lr, li = lr_ref[...], li_ref[...]
    dt = jnp.exp(ldt_ref[...])
    mag = jnp.exp(lr * dt)
    ab_re, ab_im = mag * jnp.cos(li * dt), mag * jnp.sin(li * dt)
    den = lr * lr + li * li
    g_re = ((ab_re - 1.0) * lr + ab_im * li) / den
    g_im = (ab_im * lr - (ab_re - 1.0) * li) / den
    br, bi = br_ref[...], bi_ref[...]
    ab_ref[0:1, :] = ab_re
    ab_ref[1:2, :] = ab_im
    bb_ref[0] = g_re * br - g_im * bi
    bb_ref[1] = g_re * bi + g_im * br


def _s5_prep(lam_re, lam_im, log_dt, b_re, b_im):
    row = lambda z: z.reshape(1, S5_LANES)
    ldt = jnp.broadcast_to(log_dt[:, None], (S5_GROUPS, S5_STATE))
    tr = lambda b: b.transpose(2, 0, 1).reshape(S5_CH, S5_LANES)
    return pl.pallas_call(
        _s5_prep_kernel,
        out_shape=(jax.ShapeDtypeStruct((2, S5_LANES), F32),
                   jax.ShapeDtypeStruct((2, S5_CH, S5_LANES), F32)),
        name="s5_prep",
    )(row(lam_re), row(lam_im), row(ldt), tr(b_re), tr(b_im))


def _cmul(ar, ai, br, bi):
    return ar * br - ai * bi, ar * bi + ai * br


S5_SPLIT = 4


def _s5_kernel(u_ref, ab_ref, bb_ref, cc_ref, d_ref, gw_ref, gb_ref, o_ref,
               s_ref, tab_ref, car_ref, bmat_ref, cmat_ref, *, tc, cw):
    n = S5_LANES
    nb = n // S5_SPLIT
    cb = GROUP_W // S5_SPLIT

    @pl.when(pl.program_id(0) == 0)
    def _():
        ar = jnp.broadcast_to(ab_ref[0:1, :], (SUBLANES, n))
        ai = jnp.broadcast_to(ab_ref[1:2, :], (SUBLANES, n))
        row = lax.broadcasted_iota(jnp.int32, (SUBLANES, n), 0)
        a2r, a2i = _cmul(ar, ai, ar, ai)
        a4r, a4i = _cmul(a2r, a2i, a2r, a2i)
        for k, (pr, pi, sh) in enumerate(((ar, ai, 1), (a2r, a2i, 2), (a4r, a4i, 4))):
            tab_ref[2 * k] = jnp.where(row >= sh, pr, 0.0)
            tab_ref[2 * k + 1] = jnp.where(row >= sh, pi, 0.0)
        pwr, pwi = ar, ai
        for bit, (qr, qi) in ((1, (ar, ai)), (2, (a2r, a2i)), (4, (a4r, a4i))):
            nr, ni = _cmul(pwr, pwi, qr, qi)
            sel = (row & bit) != 0
            pwr, pwi = jnp.where(sel, nr, pwr), jnp.where(sel, ni, pwi)
        tab_ref[6] = pwr
        tab_ref[7] = pwi
        car_ref[...] = jnp.zeros_like(car_ref)
        lane_grp = lax.broadcasted_iota(jnp.int32, (S5_CH, nb), 1) // S5_STATE
        for c in range(S5_SPLIT):
            st = slice(c * nb, (c + 1) * nb)
            for ref, src, sign in ((bmat_ref, bb_ref, 1.0), (cmat_ref, cc_ref, -1.0)):
                re, im = src[0, :, st], src[1, :, st] * sign
                ref[c] = jnp.concatenate(
                    [jnp.concatenate([jnp.where(lane_grp == g, re, 0.0), jnp.where(lane_grp == g, im, 0.0)], axis=1)
                     for g in range(cb // S5_CH)], axis=0).astype(BF16)

    u = u_ref[...]
    ub = u.astype(BF16)
    for c in range(S5_SPLIT):
        s_ref[:, 2 * c * nb:2 * (c + 1) * nb] = _dot(ub[:, c * cb:(c + 1) * cb], bmat_ref[c])

    cols = [(slice(c * nb + k * cw, c * nb + (k + 1) * cw),
             pl.ds(2 * c * nb + k * cw, cw), pl.ds(2 * c * nb + nb + k * cw, cw))
            for c in range(S5_SPLIT) for k in range(nb // cw)]

    def body(i, carries):
        rows = pl.ds(pl.multiple_of(i * SUBLANES, SUBLANES), SUBLANES)
        out = []
        for (st, cre, cim), (cr, ci) in zip(cols, carries):
            xr, xi = s_ref[rows, cre], s_ref[rows, cim]
            for m, sh in enumerate((1, 2, 4)):
                dr, di = _cmul(tab_ref[2 * m, :, st], tab_ref[2 * m + 1, :, st],
                               pltpu.roll(xr, sh, 0), pltpu.roll(xi, sh, 0))
                xr, xi = xr + dr, xi + di
            dr, di = _cmul(tab_ref[6, :, st], tab_ref[7, :, st],
                           jnp.broadcast_to(cr, xr.shape), jnp.broadcast_to(ci, xi.shape))
            xr, xi = xr + dr, xi + di
            s_ref[rows, cre] = xr
            s_ref[rows, cim] = xi
            out.append((xr[SUBLANES - 1:SUBLANES, :], xi[SUBLANES - 1:SUBLANES, :]))
        return tuple(out)

    carries = lax.fori_loop(0, tc // SUBLANES, body,
                            tuple((car_ref[0:1, st], car_ref[1:2, st]) for st, _, _ in cols))
    for (st, _, _), (cr, ci) in zip(cols, carries):
        car_ref[0:1, st] = cr
        car_ref[1:2, st] = ci

    y = jnp.concatenate([_dot_nt(s_ref[:, 2 * c * nb:2 * (c + 1) * nb].astype(BF16), cmat_ref[c])
                         for c in range(S5_SPLIT)], axis=1) + d_ref[...] * u
    v = _gelu_tanh(y)
    o_ref[...] = v * _sigmoid(_dot(v.astype(BF16), gw_ref[...]) + gb_ref[...])


def _s5(p, ab, bb, c_re, c_im, d, glu_w, glu_b, tc=512, cw=256):
    t = p.shape[0]
    n = S5_LANES
    cc = jnp.stack([z.transpose(1, 0, 2).reshape(S5_CH, n) for z in (c_re, c_im)])
    const = lambda shape: pl.BlockSpec(shape, lambda i: (0,) * len(shape))
    blk = (S5_SPLIT, GROUP_W // S5_SPLIT, 2 * n // S5_SPLIT)
    return pl.pallas_call(
        functools.partial(_s5_kernel, tc=tc, cw=cw),
        grid=(t // tc,),
        in_specs=[
            pl.BlockSpec((tc, GROUP_W), lambda i: (i, 0)),
            const((2, n)), const((2, S5_CH, n)), const((2, S5_CH, n)),
            const((1, GROUP_W)), const((GROUP_W, GROUP_W)), const((1, GROUP_W)),
        ],
        out_specs=pl.BlockSpec((tc, GROUP_W), lambda i: (i, 0)),
        out_shape=jax.ShapeDtypeStruct((t, GROUP_W), F32),
        scratch_shapes=[pltpu.VMEM((tc, 2 * n), F32),
                        pltpu.VMEM((8, SUBLANES, n), F32),
                        pltpu.VMEM((2, n), F32),
                        pltpu.VMEM(blk, BF16), pltpu.VMEM(blk, BF16)],
        compiler_params=_cparams("arbitrary"),
        name="s5",
    )(p, ab, bb, cc, d.reshape(1, GROUP_W), glu_w.astype(BF16), glu_b.reshape(1, GROUP_W))


def _sb_kernel(q_ref, k_ref, v_ref, qg_ref, kg_ref, tri_ref, o_ref, kn_ref, vb_ref, qn_ref, acc_ref,
               z_ref, d_ref, after_ref, w_ref, *, tq, tk, ts):
    qi = pl.program_id(1)
    nd = tq // tk
    nfull = qi * nd

    @pl.when(qi == 0)
    def _():
        kn_ref[...] = _rms(k_ref[...], kg_ref[...]).astype(BF16)
        vb_ref[...] = v_ref[...].astype(BF16)

    qn_ref[...] = (_rms(q_ref[...], qg_ref[...]) * (SB_HEAD_DIM ** -0.5)).astype(BF16)
    tri = tri_ref[...]
    diff = (lax.broadcasted_iota(jnp.int32, (ts, tk), 1) - lax.broadcasted_iota(jnp.int32, (ts, tk), 0))
    subs = tuple(range(0, tq, ts))
    krows = lambda tile: pl.ds(pl.multiple_of(tile * tk, tk), tk)

    def scores(tile):
        return _dot_nt(qn_ref[...], kn_ref[krows(tile), :])

    ntiles = nd + nfull
    hidden = -1e30

    def step(it, par, state, ds, do_q=True, do_n=True, do_w=True, do_p=True):
        cs, rs = state
        if do_q:
            z_next = scores(jnp.maximum(ntiles - 2 - it, 0))
        if do_p:
            pv = _dot(w_ref[par], vb_ref[krows(ntiles + 1 - it), :])
        if do_n:
            dvs, afters, new_rs = [], [], []
            for r0, d in zip(subs, ds):
                if d is False:
                    dvs.append(jnp.full((ts, tk), hidden, F32))
                    afters.append(jnp.zeros((ts, tk), F32))
                    new_rs.append(jnp.zeros((ts, 1), F32))
                    continue
                z = z_ref[par, r0:r0 + ts, :]
                nl = jnp.maximum(z, 0.0) + jnp.log(1.0 + jnp.exp2(jnp.abs(z) * (-LOG2E)))
                dv = z - nl
                if d is not None:
                    nl, dv = jnp.where(diff < d, nl, 0.0), jnp.where(diff < d, dv, hidden)
                dvs.append(dv)
                afters.append(_dot(nl.astype(BF16), tri))
                new_rs.append(jnp.sum(nl, axis=-1, keepdims=True))
            d_ref[par] = jnp.concatenate(dvs, axis=0)
        if do_w:
            ws = [jnp.exp(d_ref[1 - par, r0:r0 + ts, :] - after_ref[1 - par, r0:r0 + ts, :] - c).astype(BF16)
                  for r0, c in zip(subs, cs)]
            w_ref[1 - par] = jnp.concatenate(ws, axis=0)
            cs = tuple(c + r for c, r in zip(cs, rs))
        if do_n:
            after_ref[par] = jnp.concatenate(afters, axis=0)
            rs = tuple(new_rs)
        if do_q:
            z_ref[1 - par] = z_next
        if do_p:
            acc_ref[...] += pv
        return cs, rs

    z_ref[0] = scores(ntiles - 1)
    acc_ref[...] = jnp.zeros_like(acc_ref)
    zeros = tuple(jnp.zeros((ts, 1), F32) for _ in subs)
    state = (zeros, zeros)
    for it in range(nd):
        kd = (nd - 1 - it) * tk
        ds = [False if r0 - kd <= -(ts - 1) else (None if r0 - kd >= tk else r0 - kd) for r0 in subs]
        state = step(it, it % 2, state, ds, do_w=it >= 1, do_p=it >= 2)

    def two_trips(n, st):
        for par in range(2):
            st = step(nd + 2 * n + par, par, st, [None] * len(subs))
        return st

    state = lax.fori_loop(0, nfull // 2, two_trips, state)
    step(ntiles, 0, state, None, do_q=False, do_n=False)
    o_ref[...] = acc_ref[...] + _dot(w_ref[1], vb_ref[krows(0), :])


def _sb(p, q_gain, k_gain, tq=512, tk=256, ts=128):
    t = p.shape[0]
    tri = jnp.tril(jnp.ones((tk, tk), F32), -1).astype(BF16)
    return pl.pallas_call(
        functools.partial(_sb_kernel, tq=tq, tk=tk, ts=ts),
        grid=(SB_HEADS, t // tq),
        in_specs=[
            pl.BlockSpec((tq, SB_HEAD_DIM), lambda h, i: (i, 4 + h)),
            pl.BlockSpec((t, SB_HEAD_DIM), lambda h, i: (0, 8 + h)),
            pl.BlockSpec((t, SB_HEAD_DIM), lambda h, i: (0, 12 + h)),
            pl.BlockSpec((1, SB_HEAD_DIM), lambda h, i: (0, 0)),
            pl.BlockSpec((1, SB_HEAD_DIM), lambda h, i: (0, 0)),
            pl.BlockSpec((tk, tk), lambda h, i: (0, 0)),
        ],
        out_specs=pl.BlockSpec((tq, SB_HEAD_DIM), lambda h, i: (i, h)),
        out_shape=jax.ShapeDtypeStruct((t, GROUP_W), F32),
        scratch_shapes=[pltpu.VMEM((t, SB_HEAD_DIM), BF16), pltpu.VMEM((t, SB_HEAD_DIM), BF16),
                        pltpu.VMEM((tq, SB_HEAD_DIM), BF16), pltpu.VMEM((tq, SB_HEAD_DIM), F32),
                        pltpu.VMEM((2, tq, tk), F32), pltpu.VMEM((2, tq, tk), F32),
                        pltpu.VMEM((2, tq, tk), F32), pltpu.VMEM((2, tq, tk), BF16)],
        compiler_params=_cparams("arbitrary", "arbitrary"),
        name="stickbreak",
    )(p, p, p, q_gain.reshape(1, SB_HEAD_DIM), k_gain.reshape(1, SB_HEAD_DIM), tri)


RW_CHUNK = 64
(_MU_R, _MU_K, _MU_V, _MU_W, _MU_A, _MU_G, _W0, _A0, _KK, _KA, _RK, _LNG, _LNB) = range(13)


_BNN = (((2,), (1,)), ((0,), (0,)))
_BNT = (((2,), (2,)), ((0,), (0,)))


def _bmm(a, b, dims=_BNN):
    return lax.dot_general(a.astype(BF16), b.astype(BF16), dims, preferred_element_type=F32)


def _split_dot_left(m_bf16, x):
    hi = x.astype(BF16)
    lo = (x - hi.astype(F32)).astype(BF16)
    return _dot(m_bf16, hi) + _dot(m_bf16, lo)


def _rwkv_kernel(r_ref, k_ref, v_ref, m_ref, vec_ref, wa_ref, wb_ref, aa_ref, ab_ref, ga_ref, gb_ref,
                 ones_ref, tril_ref, blk_ref, o_ref, zs_ref, st_ref, *, tc):
    first = pl.program_id(0) == 0
    w = GROUP_W
    vec = lambda i: vec_ref[i:i + 1, :]

    @pl.when(first)
    def _():
        zs_ref[...] = jnp.zeros_like(zs_ref)
        st_ref[...] = jnp.zeros_like(st_ref)

    cur, prev = [], []
    for n, ref in enumerate((r_ref, k_ref, v_ref, m_ref)):
        zs_ref[n, SUBLANES - 1:SUBLANES, :] = zs_ref[n, SUBLANES + tc - 1:SUBLANES + tc, :]
        z = ref[...]
        zs_ref[n, SUBLANES:SUBLANES + tc, :] = z
        cur.append(z)
        prev.append(zs_ref[n, SUBLANES - 1:SUBLANES - 1 + tc, :])
    lerp = lambda n, mu: cur[n] + (prev[n] - cur[n]) * vec(mu)
    r, k, v = lerp(0, _MU_R), lerp(1, _MU_K), lerp(2, _MU_V)
    xw, xa, xg = lerp(3, _MU_W), lerp(3, _MU_A), lerp(3, _MU_G)

    lora = lambda x, a_ref, b_ref, f: _dot(f(_dot(x.astype(BF16), a_ref[...])).astype(BF16), b_ref[...])
    w_log = -_softplus(-(vec(_W0) + lora(xw, wa_ref, wb_ref, jnp.tanh))) - 0.5
    lw = -jnp.exp(w_log)
    a = _sigmoid(vec(_A0) + lora(xa, aa_ref, ab_ref, lambda z: z))
    g = lora(xg, ga_ref, gb_ref, _sigmoid)
    ones = ones_ref[...]
    kk = k * vec(_KK)
    kk = kk * lax.rsqrt(_split_dot(kk * kk, ones) + 1e-12)
    k = k * (1.0 + (a - 1.0) * vec(_KA))
    bonus = _split_dot(r * k * vec(_RK), ones) * v
    a_in, b_in = -kk, kk * a

    L = RW_CHUNK
    nch, nhp = tc // L, w // 128
    cum = _split_dot_left(tril_ref[...], lw)
    cum_l = _split_dot_left(blk_ref[...], lw)
    g_inv, g_end = jnp.exp(-cum), jnp.exp(cum_l - cum)
    head0 = lax.broadcasted_iota(jnp.int32, (L, 128), 1) < RW_HEAD_DIM

    def stacked(x, split_heads=True):
        out = []
        for c in range(nch):
            for hp in range(nhp):
                z = x[c * L:(c + 1) * L, hp * 128:(hp + 1) * 128]
                out.append(jnp.concatenate([jnp.where(head0, z, 0.0), jnp.where(head0, 0.0, z)], axis=0)
                           if split_heads else jnp.concatenate([z, z], axis=0))
        return jnp.stack(out)

    xa_, xr_ = stacked(a_in * jnp.exp(cum - lw)).astype(BF16), stacked(r * jnp.exp(cum)).astype(BF16)
    xb_, xk_, xv_ = stacked(b_in * g_inv).astype(BF16), stacked(k * g_inv).astype(BF16), stacked(v).astype(BF16)
    xbl_t = jnp.swapaxes(stacked(b_in * g_end), 1, 2).astype(BF16)
    xkl_t = jnp.swapaxes(stacked(k * g_end), 1, 2).astype(BF16)
    g_col = jnp.swapaxes(stacked(jnp.exp(cum_l), split_heads=False), 1, 2)

    ri = lax.broadcasted_iota(jnp.int32, (2 * L, 2 * L), 0)
    ci = lax.broadcasted_iota(jnp.int32, (2 * L, 2 * L), 1)
    strict, incl = ri > ci, ri >= ci
    eye = (ri == ci).astype(F32)
    pm = _bmm(jnp.concatenate([xa_, xr_], axis=1), jnp.concatenate([xb_, xk_], axis=1), _BNT)
    n_ab = jnp.where(strict, pm[:, :2 * L, :2 * L], 0.0)
    a_ak = jnp.where(strict, pm[:, :2 * L, 2 * L:], 0.0)
    a_rb = jnp.where(incl, pm[:, 2 * L:, :2 * L], 0.0).astype(BF16)
    a_rk = jnp.where(incl, pm[:, 2 * L:, 2 * L:], 0.0)
    tinv, npow = eye + n_ab, n_ab
    for _ in range(int(math.log2(L)) - 1):
        npow = _bmm(npow, npow)
        tinv = tinv + _bmm(npow, tinv)
    wu = _bmm(tinv, jnp.concatenate([xa_, _bmm(a_ak, xv_).astype(BF16)], axis=2))
    wx = jnp.concatenate([wu[:, :, :128].astype(BF16), xr_], axis=1)
    u0 = wu[:, :, 128:]
    y0 = _bmm(a_rk, xv_)
    s0 = _bmm(xkl_t, xv_)

    st = st_ref[...]
    ys = []
    for c in range(nch):
        sl = slice(c * nhp, (c + 1) * nhp)
        gs = _bmm(wx[sl], st)
        u = (gs[:, :2 * L] + u0[sl]).astype(BF16)
        yst = gs[:, 2 * L:] + _bmm(a_rb[sl], u) + y0[sl]
        st = g_col[sl] * st + _bmm(xbl_t[sl], u) + s0[sl]
        ys.append(jnp.concatenate([yst[hp, :L] + yst[hp, L:] for hp in range(nhp)], axis=1))
    st_ref[...] = st

    y = jnp.concatenate(ys, axis=0)
    inv_n = 1.0 / RW_HEAD_DIM
    mean = _split_dot(y, ones) * inv_n
    yc = y - mean
    var = _split_dot(yc * yc, ones) * inv_n
    y = yc * lax.rsqrt(var + RW_GN_EPS) * vec(_LNG) + vec(_LNB)
    o_ref[...] = (y + bonus) * g


def _rwkv(p, mu, w0, w_a, w_b, a0, a_a, a_b, g_a, g_b, k_k, k_a, r_k, ln_g, ln_b, tc=256):
    t = p.shape[0]
    w = GROUP_W
    vecs = jnp.concatenate([mu, jnp.stack([w0, a0, k_k, k_a, r_k.reshape(w), ln_g, ln_b]),
                            jnp.zeros((3, w), F32)], axis=0)
    head = jnp.arange(w) // RW_HEAD_DIM
    ones = (head[:, None] == head[None, :]).astype(BF16)
    chunk = jnp.arange(tc) // RW_CHUNK
    blk = chunk[:, None] == chunk[None, :]
    tril = jnp.logical_and(blk, jnp.arange(tc)[:, None] >= jnp.arange(tc)[None, :])
    const = lambda shape: pl.BlockSpec(shape, lambda i: (0,) * len(shape))
    col = lambda c: pl.BlockSpec((tc, w), lambda i: (i, c))
    bf = lambda z: z.astype(BF16)
    return pl.pallas_call(
        functools.partial(_rwkv_kernel, tc=tc),
        grid=(t // tc,),
        in_specs=[col(4), col(5), col(6), col(7), const((16, w)),
                  const(w_a.shape), const(w_b.shape), const(a_a.shape), const(a_b.shape),
                  const(g_a.shape), const(g_b.shape), const((w, w)), const((tc, tc)), const((tc, tc))],
        out_specs=pl.BlockSpec((tc, w), lambda i: (i, 0)),
        out_shape=jax.ShapeDtypeStruct((t, w), F32),
        scratch_shapes=[pltpu.VMEM((4, tc + SUBLANES, w), F32), pltpu.VMEM((w // 128, 128, 128), F32)],
        compiler_params=_cparams("arbitrary"),
        name="rwkv7",
    )(p, p, p, p, vecs, bf(w_a), bf(w_b), bf(a_a), bf(a_b), bf(g_a), bf(g_b), ones, bf(tril), bf(blk))


def _rglru_kernel(gate_ref, x_ref, cw_ref, vec_ref, wx_ref, wa_ref, o_ref,
                  xs_ref, a_s, b_s, car_ref, *, tc):
    first = pl.program_id(0) == 0
    halo = SUBLANES

    @pl.when(first)
    def _():
        xs_ref[...] = jnp.zeros_like(xs_ref)
        car_ref[...] = jnp.zeros_like(car_ref)

    xs_ref[0:halo, :] = xs_ref[tc:tc + halo, :]
    xs_ref[halo:halo + tc, :] = x_ref[...]
    xc = vec_ref[0:1, :]
    for j in range(RG_CONV):
        off = halo - (RG_CONV - 1) + j
        xc = xc + xs_ref[off:off + tc, :] * cw_ref[j:j + 1, :]
    xcb = xc.astype(BF16)
    gate_x = _sigmoid(_dot(xcb, wx_ref[...]) + vec_ref[1:2, :])
    gate_a = _sigmoid(_dot(xcb, wa_ref[...]) + vec_ref[2:3, :])
    lam = vec_ref[3:4, :]
    log_a = RG_C * gate_a * (-_softplus(-lam))
    a = jnp.exp(log_a)
    mult = jnp.sqrt(1.0 - a * a)
    t_idx = lax.broadcasted_iota(jnp.int32, (tc, GROUP_W), 0)
    mult = jnp.where(jnp.logical_and(first, t_idx == 0), 1.0, mult)
    a_s[...] = a
    b_s[...] = xc * gate_x * mult

    row = lax.broadcasted_iota(jnp.int32, (SUBLANES, GROUP_W), 0)

    def body(i, carry):
        rows = pl.ds(pl.multiple_of(i * SUBLANES, SUBLANES), SUBLANES)
        ab, hb = a_s[rows, :], b_s[rows, :]
        for sh in (1, 2, 4):
            ash = jnp.where(row >= sh, pltpu.roll(ab, sh, 0), 1.0)
            hsh = jnp.where(row >= sh, pltpu.roll(hb, sh, 0), 0.0)
            hb = hb + ab * hsh
            ab = ab * ash
        hb = hb + ab * jnp.broadcast_to(carry, hb.shape)
        b_s[rows, :] = hb
        return hb[SUBLANES - 1:SUBLANES, :]

    car_ref[...] = lax.fori_loop(0, tc // SUBLANES, body, car_ref[...], unroll=2)
    o_ref[...] = _gelu_tanh(gate_ref[...]) * b_s[...]


def _rglru(p, conv_w, conv_b, w_x, b_x, w_a, b_a, lam, tc=512):
    t = p.shape[0]
    w = GROUP_W
    blk = jnp.arange(w) // (w // w_x.shape[0])
    same = blk[:, None] == blk[None, :]
    bdiag = lambda m: jnp.where(same, jnp.tile(m.reshape(w, -1), (1, m.shape[0])), 0.0).astype(BF16)
    vecs = jnp.concatenate([jnp.stack([conv_b, b_x, b_a, lam]), jnp.zeros((4, w), F32)], axis=0)
    const = lambda shape: pl.BlockSpec(shape, lambda i: (0,) * len(shape))
    return pl.pallas_call(
        functools.partial(_rglru_kernel, tc=tc),
        grid=(t // tc,),
        in_specs=[pl.BlockSpec((tc, w), lambda i: (i, 8)), pl.BlockSpec((tc, w), lambda i: (i, 9)),
                  const((RG_CONV, w)), const((8, w)), const((w, w)), const((w, w))],
        out_specs=pl.BlockSpec((tc, w), lambda i: (i, 0)),
        out_shape=jax.ShapeDtypeStruct((t, w), F32),
        scratch_shapes=[pltpu.VMEM((tc + SUBLANES, w), F32), pltpu.VMEM((tc, w), F32),
                        pltpu.VMEM((tc, w), F32), pltpu.VMEM((1, w), F32)],
        compiler_params=_cparams("arbitrary"),
        name="rglru",
    )(p, p, conv_w, vecs, bdiag(w_x), bdiag(w_a))


def kernel(x, ffn1_norm, ffn1_w1, ffn1_w3, ffn1_w2, mix_norm, w_in, s5_lambda_re, s5_lambda_im, s5_log_dt, s5_b_re, s5_b_im, s5_c_re, s5_c_im, s5_d, s5_glu_w, s5_glu_b, sb_q_gain, sb_k_gain, rw_mu, rw_w0, rw_w_a, rw_w_b, rw_a0, rw_a_a, rw_a_b, rw_g_a, rw_g_b, rw_k_k, rw_k_a, rw_r_k, rw_ln_g, rw_ln_b, rg_conv_w, rg_conv_b, rg_w_x, rg_b_x, rg_w_a, rg_b_a, rg_lambda, out_norm, w_out, ffn2_norm, ffn2_w1, ffn2_w3, ffn2_w2):
    bn, t, _ = x.shape
    assert bn == 1
    x = x.reshape(t, D_MODEL)
    bf = lambda z: z.astype(BF16)
    for l in range(ffn1_norm.shape[0]):
        x = _ffn(x, ffn1_norm[l], ffn1_w1, ffn1_w3, ffn1_w2, l)
        p = _inproj(x, mix_norm[l], w_in, l)
        ab, bb = _s5_prep(s5_lambda_re[l], s5_lambda_im[l], s5_log_dt[l], s5_b_re[l], s5_b_im[l])
        y_a = _s5(p, ab, bb, s5_c_re[l], s5_c_im[l], s5_d[l], s5_glu_w[l], s5_glu_b[l])
        y_b = _sb(p, sb_q_gain[l], sb_k_gain[l])
        y_c = _rwkv(p, rw_mu[l], rw_w0[l], rw_w_a[l], rw_w_b[l], rw_a0[l], rw_a_a[l], rw_a_b[l],
                    rw_g_a[l], rw_g_b[l], rw_k_k[l], rw_k_a[l], rw_r_k[l], rw_ln_g[l], rw_ln_b[l])
        y_d = _rglru(p, rg_conv_w[l], rg_conv_b[l], rg_w_x[l], rg_b_x[l], rg_w_a[l], rg_b_a[l],
                     rg_lambda[l])
        x = _outproj(x, (y_a, y_b, y_c, y_d), out_norm[l], bf(w_out[l]))
        x = _ffn(x, ffn2_norm[l], ffn2_w1, ffn2_w3, ffn2_w2, l)
    return x.reshape(bn, t, D_MODEL)
```

```python
import functools
import math

import jax
import jax.numpy as jnp
from jax import lax
from jax.experimental import pallas as pl
from jax.experimental.pallas import tpu as pltpu

F32 = jnp.float32
BF16 = jnp.bfloat16

D_MODEL = 2048
GROUP_W = 512
D_FF = 5632
D_IN = 10 * GROUP_W
S5_CH = 16
S5_GROUPS = 32
S5_STATE = 64
S5_LANES = S5_GROUPS * S5_STATE
SB_HEADS = 4
SB_HEAD_DIM = 128
RW_HEAD_DIM = 64
RW_GN_EPS = 64e-5
RG_CONV = 4
RG_C = 8.0
NORM_EPS = 1e-6
LOG2E = 1.4426950408889634

SUBLANES = 8
MXU_DIM = 256
VMEM_LIMIT = 56 * 1024 * 1024
FFN_VMEM_LIMIT = 60 * 1024 * 1024

HI = lax.Precision.HIGHEST


def _cparams(*sem):
    return pltpu.CompilerParams(dimension_semantics=sem, vmem_limit_bytes=VMEM_LIMIT)


def _dot(a, b):
    return jnp.dot(a, b, preferred_element_type=F32)


def _dot_nt(a, b):
    return lax.dot_general(a, b, (((1,), (1,)), ((), ())), preferred_element_type=F32)


def _split_dot(x, m_bf16):
    hi = x.astype(BF16)
    lo = (x - hi.astype(F32)).astype(BF16)
    return _dot(hi, m_bf16) + _dot(lo, m_bf16)


def _rms(x, g):
    return x * lax.rsqrt(jnp.mean(x * x, axis=-1, keepdims=True) + NORM_EPS) * g


def _sigmoid(x):
    return 1.0 / (1.0 + jnp.exp(-x))


def _softplus(x):
    return jnp.maximum(x, 0.0) + jnp.log1p(jnp.exp(-jnp.abs(x)))


def _gelu_tanh(x):
    c = math.sqrt(2.0 / math.pi)
    return 0.5 * x * (1.0 + jnp.tanh(c * (x + 0.044715 * (x * x * x))))


def _ffn_kernel(x_hbm, g_ref, w1_ref, w3_ref, w2_ref, o_hbm, acc_ref, h_ref, ld_sem, st_sem, *, tm):
    i, j = pl.program_id(0), pl.program_id(1)
    ni, nj = pl.num_programs(0), pl.num_programs(1)
    slot = i % 2
    rows = lambda tile: pl.ds(pl.multiple_of(tile * tm, tm), tm)
    load = lambda tile, s: pltpu.make_async_copy(x_hbm.at[rows(tile), :], acc_ref.at[s], ld_sem.at[s])
    store = lambda tile, s: pltpu.make_async_copy(acc_ref.at[s], o_hbm.at[rows(tile), :], st_sem.at[s])

    @pl.when(jnp.logical_and(i == 0, j == 0))
    def _():
        load(0, 0).start()

    @pl.when(j == 0)
    def _():
        load(i, slot).wait()
        h_ref[...] = _rms(acc_ref[slot], g_ref[...]).astype(BF16)

    @pl.when(j == 1)
    def _():
        @pl.when(i >= 1)
        def _():
            store(i - 1, 1 - slot).wait()

        @pl.when(i + 1 < ni)
        def _():
            load(i + 1, 1 - slot).start()

    h = h_ref[...]
    a = _dot(h, w1_ref[...].astype(BF16))
    b = _dot(h, w3_ref[...].astype(BF16))
    act = (0.5 * (a * _sigmoid(a)) * b).astype(BF16)
    acc_ref[slot] += _dot(act, w2_ref[...].astype(BF16))

    @pl.when(j == nj - 1)
    def _():
        store(i, slot).start()

        @pl.when(i == ni - 1)
        def _():
            store(i, slot).wait()


def _ffn(x, g, w1, w3, w2, l, tm=1024, tf=512):
    t = x.shape[0]
    assert D_FF // tf >= 2 and t // tm >= 2
    return pl.pallas_call(
        functools.partial(_ffn_kernel, tm=tm),
        grid=(t // tm, D_FF // tf),
        in_specs=[
            pl.BlockSpec(memory_space=pl.ANY),
            pl.BlockSpec((1, D_MODEL), lambda i, j: (0, 0)),
            pl.BlockSpec((None, D_MODEL, tf), lambda i, j: (l, 0, j)),
            pl.BlockSpec((None, D_MODEL, tf), lambda i, j: (l, 0, j)),
            pl.BlockSpec((None, tf, D_MODEL), lambda i, j: (l, j, 0)),
        ],
        out_specs=pl.BlockSpec(memory_space=pl.ANY),
        out_shape=jax.ShapeDtypeStruct((t, D_MODEL), F32),
        scratch_shapes=[pltpu.VMEM((2, tm, D_MODEL), F32), pltpu.VMEM((tm, D_MODEL), BF16),
                        pltpu.SemaphoreType.DMA((2,)), pltpu.SemaphoreType.DMA((2,))],
        compiler_params=pltpu.CompilerParams(dimension_semantics=("arbitrary", "arbitrary"),
                                             vmem_limit_bytes=FFN_VMEM_LIMIT),
        name="ffn",
    )(x, g.reshape(1, D_MODEL), w1, w3, w2)


def _inproj_kernel(x_ref, g_ref, w_ref, o_ref, h_ref):
    @pl.when(pl.program_id(1) == 0)
    def _():
        h_ref[...] = _rms(x_ref[...], g_ref[...]).astype(BF16)

    o_ref[...] = _dot(h_ref[...], w_ref[...].astype(BF16))


def _inproj(x, g, w, l, tm=2048, tn=512):
    t = x.shape[0]
    return pl.pallas_call(
        _inproj_kernel,
        grid=(t // tm, D_IN // tn),
        in_specs=[
            pl.BlockSpec((tm, D_MODEL), lambda i, j: (i, 0), pipeline_mode=pl.Buffered(1)),
            pl.BlockSpec((1, D_MODEL), lambda i, j: (0, 0)),
            pl.BlockSpec((None, D_MODEL, tn), lambda i, j: (l, 0, j)),
        ],
        out_specs=pl.BlockSpec((tm, tn), lambda i, j: (i, j)),
        out_shape=jax.ShapeDtypeStruct((t, D_IN), F32),
        scratch_shapes=[pltpu.VMEM((tm, D_MODEL), BF16)],
        compiler_params=_cparams("parallel", "arbitrary"),
        name="inproj",
    )(x, g.reshape(1, D_MODEL), w)


def _s5_prep_kernel(lr_ref, li_ref, ldt_ref, br_ref, bi_ref, ab_ref, bb_ref):
    lr, li = lr_ref[...], li_ref[...]
    dt = jnp.exp(ldt_ref[...])
    mag = jnp.exp(lr * dt)
    ab_re, ab_im = mag * jnp.cos(li * dt), mag * jnp.sin(li * dt)
    den = lr * lr + li * li
    g_re = ((ab_re - 1.0) * lr + ab_im * li) / den
    g_im = (ab_im * lr - (ab_re - 1.0) * li) / den
    br, bi = br_ref[...], bi_ref[...]
    ab_ref[0:1, :] = ab_re
    ab_ref[1:2, :] = ab_im
    bb_ref[0] = g_re * br - g_im * bi
    bb_ref[1] = g_re * bi + g_im * br


def _s5_prep(lam_re, lam_im, log_dt, b_re, b_im):
    row = lambda z: z.reshape(1, S5_LANES)
    ldt = jnp.broadcast_to(log_dt[:, None], (S5_GROUPS, S5_STATE))
    tr = lambda b: b.transpose(2, 0, 1).reshape(S5_CH, S5_LANES)
    return pl.pallas_call(
        _s5_prep_kernel,
        out_shape=(jax.ShapeDtypeStruct((2, S5_LANES), F32),
                   jax.ShapeDtypeStruct((2, S5_CH, S5_LANES), F32)),
        name="s5_prep",
    )(row(lam_re), row(lam_im), row(ldt), tr(b_re), tr(b_im))


def _cmul(ar, ai, br, bi):
    return ar * br - ai * bi, ar * bi + ai * br


S5_SPLIT = 4


def _s5_kernel(u_ref, ab_ref, bb_ref, cc_ref, d_ref, gw_ref, gb_ref, o_ref,
               s_ref, tab_ref, car_ref, bmat_ref, cmat_ref, *, tc, cw):
    n = S5_LANES
    nb = n // S5_SPLIT
    cb = GROUP_W // S5_SPLIT

    @pl.when(pl.program_id(0) == 0)
    def _():
        ar = jnp.broadcast_to(ab_ref[0:1, :], (SUBLANES, n))
        ai = jnp.broadcast_to(ab_ref[1:2, :], (SUBLANES, n))
        row = lax.broadcasted_iota(jnp.int32, (SUBLANES, n), 0)
        a2r, a2i = _cmul(ar, ai, ar, ai)
        a4r, a4i = _cmul(a2r, a2i, a2r, a2i)
        for k, (pr, pi, sh) in enumerate(((ar, ai, 1), (a2r, a2i, 2), (a4r, a4i, 4))):
            tab_ref[2 * k] = jnp.where(row >= sh, pr, 0.0)
            tab_ref[2 * k + 1] = jnp.where(row >= sh, pi, 0.0)
        pwr, pwi = ar, ai
        for bit, (qr, qi) in ((1, (ar, ai)), (2, (a2r, a2i)), (4, (a4r, a4i))):
            nr, ni = _cmul(pwr, pwi, qr, qi)
            sel = (row & bit) != 0
            pwr, pwi = jnp.where(sel, nr, pwr), jnp.where(sel, ni, pwi)
        tab_ref[6] = pwr
        tab_ref[7] = pwi
        car_ref[...] = jnp.zeros_like(car_ref)
        lane_grp = lax.broadcasted_iota(jnp.int32, (S5_CH, nb), 1) // S5_STATE
        for c in range(S5_SPLIT):
            st = slice(c * nb, (c + 1) * nb)
            for ref, src, sign in ((bmat_ref, bb_ref, 1.0), (cmat_ref, cc_ref, -1.0)):
                re, im = src[0, :, st], src[1, :, st] * sign
                ref[c] = jnp.concatenate(
                    [jnp.concatenate([jnp.where(lane_grp == g, re, 0.0), jnp.where(lane_grp == g, im, 0.0)], axis=1)
                     for g in range(cb // S5_CH)], axis=0).astype(BF16)

    u = u_ref[...]
    ub = u.astype(BF16)
    for c in range(S5_SPLIT):
        s_ref[:, 2 * c * nb:2 * (c + 1) * nb] = _dot(ub[:, c * cb:(c + 1) * cb], bmat_ref[c])

    cols = [(slice(c * nb + k * cw, c * nb + (k + 1) * cw),
             pl.ds(2 * c * nb + k * cw, cw), pl.ds(2 * c * nb + nb + k * cw, cw))
            for c in range(S5_SPLIT) for k in range(nb // cw)]

    def body(i, carries):
        rows = pl.ds(pl.multiple_of(i * SUBLANES, SUBLANES), SUBLANES)
        out = []
        for (st, cre, cim), (cr, ci) in zip(cols, carries):
            xr, xi = s_ref[rows, cre], s_ref[rows, cim]
            for m, sh in enumerate((1, 2, 4)):
                dr, di = _cmul(tab_ref[2 * m, :, st], tab_ref[2 * m + 1, :, st],
                               pltpu.roll(xr, sh, 0), pltpu.roll(xi, sh, 0))
                xr, xi = xr + dr, xi + di
            dr, di = _cmul(tab_ref[6, :, st], tab_ref[7, :, st],
                           jnp.broadcast_to(cr, xr.shape), jnp.broadcast_to(ci, xi.shape))
            xr, xi = xr + dr, xi + di
            s_ref[rows, cre] = xr
            s_ref[rows, cim] = xi
            out.append((xr[SUBLANES - 1:SUBLANES, :], xi[SUBLANES - 1:SUBLANES, :]))
        return tuple(out)

    carries = lax.fori_loop(0, tc // SUBLANES, body,
                            tuple((car_ref[0:1, st], car_ref[1:2, st]) for st, _, _ in cols))
    for (st, _, _), (cr, ci) in zip(cols, carries):
        car_ref[0:1, st] = cr
        car_ref[1:2, st] = ci

    y = jnp.concatenate([_dot_nt(s_ref[:, 2 * c * nb:2 * (c + 1) * nb].astype(BF16), cmat_ref[c])
                         for c in range(S5_SPLIT)], axis=1) + d_ref[...] * u
    v = _gelu_tanh(y)
    o_ref[...] = v * _sigmoid(_dot(v.astype(BF16), gw_ref[...]) + gb_ref[...])


def _s5(p, ab, bb, c_re, c_im, d, glu_w, glu_b, tc=512, cw=256):
    t = p.shape[0]
    n = S5_LANES
    cc = jnp.stack([z.transpose(1, 0, 2).reshape(S5_CH, n) for z in (c_re, c_im)])
    const = lambda shape: pl.BlockSpec(shape, lambda i: (0,) * len(shape))
    blk = (S5_SPLIT, GROUP_W // S5_SPLIT, 2 * n // S5_SPLIT)
    return pl.pallas_call(
        functools.partial(_s5_kernel, tc=tc, cw=cw),
        grid=(t // tc,),
        in_specs=[
            pl.BlockSpec((tc, GROUP_W), lambda i: (i, 0)),
            const((2, n)), const((2, S5_CH, n)), const((2, S5_CH, n)),
            const((1, GROUP_W)), const((GROUP_W, GROUP_W)), const((1, GROUP_W)),
        ],
        out_specs=pl.BlockSpec((tc, GROUP_W), lambda i: (i, 0)),
        out_shape=jax.ShapeDtypeStruct((t, GROUP_W), F32),
        scratch_shapes=[pltpu.VMEM((tc, 2 * n), F32),
                        pltpu.VMEM((8, SUBLANES, n), F32),
                        pltpu.VMEM((2, n), F32),
                        pltpu.VMEM(blk, BF16), pltpu.VMEM(blk, BF16)],
        compiler_params=_cparams("arbitrary"),
        name="s5",
    )(p, ab, bb, cc, d.reshape(1, GROUP_W), glu_w.astype(BF16), glu_b.reshape(1, GROUP_W))


def _sb_kernel(q_ref, k_ref, v_ref, qg_ref, kg_ref, tri_ref, o_ref, kn_ref, vb_ref, qn_ref, acc_ref,
               z_ref, d_ref, after_ref, w_ref, *, tq, tk, ts):
    qi = pl.program_id(1)
    nd = tq // tk
    nfull = qi * nd

    @pl.when(qi == 0)
    def _():
        kn_ref[...] = _rms(k_ref[...], kg_ref[...]).astype(BF16)
        vb_ref[...] = v_ref[...].astype(BF16)

    qn_ref[...] = (_rms(q_ref[...], qg_ref[...]) * (SB_HEAD_DIM ** -0.5)).astype(BF16)
    tri = tri_ref[...]
    diff = (lax.broadcasted_iota(jnp.int32, (ts, tk), 1) - lax.broadcasted_iota(jnp.int32, (ts, tk), 0))
    subs = tuple(range(0, tq, ts))
    krows = lambda tile: pl.ds(pl.multiple_of(tile * tk, tk), tk)

    def scores(tile):
        return _dot_nt(qn_ref[...], kn_ref[krows(tile), :])

    ntiles = nd + nfull
    hidden = -1e30

    def step(it, par, state, ds, do_q=True, do_n=True, do_w=True, do_p=True):
        cs, rs = state
        if do_q:
            z_next = scores(jnp.maximum(ntiles - 2 - it, 0))
        if do_p:
            pv = _dot(w_ref[par], vb_ref[krows(ntiles + 1 - it), :])
        if do_n:
            dvs, afters, new_rs = [], [], []
            for r0, d in zip(subs, ds):
                if d is False:
                    dvs.append(jnp.full((ts, tk), hidden, F32))
                    afters.append(jnp.zeros((ts, tk), F32))
                    new_rs.append(jnp.zeros((ts, 1), F32))
                    continue
                z = z_ref[par, r0:r0 + ts, :]
                nl = jnp.maximum(z, 0.0) + jnp.log(1.0 + jnp.exp2(jnp.abs(z) * (-LOG2E)))
                dv = z - nl
                if d is not None:
                    nl, dv = jnp.where(diff < d, nl, 0.0), jnp.where(diff < d, dv, hidden)
                dvs.append(dv)
                afters.append(_dot(nl.astype(BF16), tri))
                new_rs.append(jnp.sum(nl, axis=-1, keepdims=True))
            d_ref[par] = jnp.concatenate(dvs, axis=0)
        if do_w:
            ws = [jnp.exp(d_ref[1 - par, r0:r0 + ts, :] - after_ref[1 - par, r0:r0 + ts, :] - c).astype(BF16)
                  for r0, c in zip(subs, cs)]
            w_ref[1 - par] = jnp.concatenate(ws, axis=0)
            cs = tuple(c + r for c, r in zip(cs, rs))
        if do_n:
            after_ref[par] = jnp.concatenate(afters, axis=0)
            rs = tuple(new_rs)
        if do_q:
            z_ref[1 - par] = z_next
        if do_p:
            acc_ref[...] += pv
        return cs, rs

    z_ref[0] = scores(ntiles - 1)
    acc_ref[...] = jnp.zeros_like(acc_ref)
    zeros = tuple(jnp.zeros((ts, 1), F32) for _ in subs)
    state = (zeros, zeros)
    for it in range(nd):
        kd = (nd - 1 - it) * tk
        ds = [False if r0 - kd <= -(ts - 1) else (None if r0 - kd >= tk else r0 - kd) for r0 in subs]
        state = step(it, it % 2, state, ds, do_w=it >= 1, do_p=it >= 2)

    def two_trips(n, st):
        for par in range(2):
            st = step(nd + 2 * n + par, par, st, [None] * len(subs))
        return st

    state = lax.fori_loop(0, nfull // 2, two_trips, state)
    step(ntiles, 0, state, None, do_q=False, do_n=False)
    o_ref[...] = acc_ref[...] + _dot(w_ref[1], vb_ref[krows(0), :])


def _sb(p, q_gain, k_gain, tq=512, tk=256, ts=128):
    t = p.shape[0]
    tri = jnp.tril(jnp.ones((tk, tk), F32), -1).astype(BF16)
    return pl.pallas_call(
        functools.partial(_sb_kernel, tq=tq, tk=tk, ts=ts),
        grid=(SB_HEADS, t // tq),
        in_specs=[
            pl.BlockSpec((tq, SB_HEAD_DIM), lambda h, i: (i, 4 + h)),
            pl.BlockSpec((t, SB_HEAD_DIM), lambda h, i: (0, 8 + h)),
            pl.BlockSpec((t, SB_HEAD_DIM), lambda h, i: (0, 12 + h)),
            pl.BlockSpec((1, SB_HEAD_DIM), lambda h, i: (0, 0)),
            pl.BlockSpec((1, SB_HEAD_DIM), lambda h, i: (0, 0)),
            pl.BlockSpec((tk, tk), lambda h, i: (0, 0)),
        ],
        out_specs=pl.BlockSpec((tq, SB_HEAD_DIM), lambda h, i: (i, h)),
        out_shape=jax.ShapeDtypeStruct((t, GROUP_W), F32),
        scratch_shapes=[pltpu.VMEM((t, SB_HEAD_DIM), BF16), pltpu.VMEM((t, SB_HEAD_DIM), BF16),
                        pltpu.VMEM((tq, SB_HEAD_DIM), BF16), pltpu.VMEM((tq, SB_HEAD_DIM), F32),
                        pltpu.VMEM((2, tq, tk), F32), pltpu.VMEM((2, tq, tk), F32),
                        pltpu.VMEM((2, tq, tk), F32), pltpu.VMEM((2, tq, tk), BF16)],
        compiler_params=_cparams("arbitrary", "arbitrary"),
        name="stickbreak",
    )(p, p, p, q_gain.reshape(1, SB_HEAD_DIM), k_gain.reshape(1, SB_HEAD_DIM), tri)


RW_CHUNK = 64
(_MU_R, _MU_K, _MU_V, _MU_W, _MU_A, _MU_G, _W0, _A0, _KK, _KA, _RK, _LNG, _LNB) = range(13)


_BNN = (((2,), (1,)), ((0,), (0,)))
_BNT = (((2,), (2,)), ((0,), (0,)))


def _bmm(a, b, dims=_BNN):
    return lax.dot_general(a.astype(BF16), b.astype(BF16), dims, preferred_element_type=F32)


def _split_dot_left(m_bf16, x):
    hi = x.astype(BF16)
    lo = (x - hi.astype(F32)).astype(BF16)
    return _dot(m_bf16, hi) + _dot(m_bf16, lo)


def _rwkv_kernel(r_ref, k_ref, v_ref, m_ref, vec_ref, wa_ref, wb_ref, aa_ref, ab_ref, ga_ref, gb_ref,
                 ones_ref, tril_ref, blk_ref, o_ref, zs_ref, st_ref, *, tc):
    first = pl.program_id(0) == 0
    w = GROUP_W
    vec = lambda i: vec_ref[i:i + 1, :]

    @pl.when(first)
    def _():
        zs_ref[...] = jnp.zeros_like(zs_ref)
        st_ref[...] = jnp.zeros_like(st_ref)

    cur, prev = [], []
    for n, ref in enumerate((r_ref, k_ref, v_ref, m_ref)):
        zs_ref[n, SUBLANES - 1:SUBLANES, :] = zs_ref[n, SUBLANES + tc - 1:SUBLANES + tc, :]
        z = ref[...]
        zs_ref[n, SUBLANES:SUBLANES + tc, :] = z
        cur.append(z)
        prev.append(zs_ref[n, SUBLANES - 1:SUBLANES - 1 + tc, :])
    lerp = lambda n, mu: cur[n] + (prev[n] - cur[n]) * vec(mu)
    r, k, v = lerp(0, _MU_R), lerp(1, _MU_K), lerp(2, _MU_V)
    xw, xa, xg = lerp(3, _MU_W), lerp(3, _MU_A), lerp(3, _MU_G)

    lora = lambda x, a_ref, b_ref, f: _dot(f(_dot(x.astype(BF16), a_ref[...])).astype(BF16), b_ref[...])
    w_log = -_softplus(-(vec(_W0) + lora(xw, wa_ref, wb_ref, jnp.tanh))) - 0.5
    lw = -jnp.exp(w_log)
    a = _sigmoid(vec(_A0) + lora(xa, aa_ref, ab_ref, lambda z: z))
    g = lora(xg, ga_ref, gb_ref, _sigmoid)
    ones = ones_ref[...]

    def head_sums(x):
        return jnp.concatenate([_split_dot(x[:, c:c + MXU_DIM], ones) for c in range(0, w, MXU_DIM)], axis=1)

    kk = k * vec(_KK)
    kk = kk * lax.rsqrt(head_sums(kk * kk) + 1e-12)
    k = k * (1.0 + (a - 1.0) * vec(_KA))
    bonus = head_sums(r * k * vec(_RK)) * v
    a_in, b_in = -kk, kk * a

    L = RW_CHUNK
    nch, nhp = tc // L, w // 128
    cum = _split_dot_left(tril_ref[...], lw)
    cum_l = _split_dot_left(blk_ref[...], lw)
    g_inv, g_end = jnp.exp(-cum), jnp.exp(cum_l - cum)
    head0 = lax.broadcasted_iota(jnp.int32, (L, 128), 1) < RW_HEAD_DIM

    def stacked(x, split_heads=True):
        out = []
        for c in range(nch):
            for hp in range(nhp):
                z = x[c * L:(c + 1) * L, hp * 128:(hp + 1) * 128]
                out.append(jnp.concatenate([jnp.where(head0, z, 0.0), jnp.where(head0, 0.0, z)], axis=0)
                           if split_heads else jnp.concatenate([z, z], axis=0))
        return jnp.stack(out)

    xa_, xr_ = stacked(a_in * jnp.exp(cum - lw)).astype(BF16), stacked(r * jnp.exp(cum)).astype(BF16)
    xb_, xk_, xv_ = stacked(b_in * g_inv).astype(BF16), stacked(k * g_inv).astype(BF16), stacked(v).astype(BF16)
    xbl_t = jnp.swapaxes(stacked(b_in * g_end), 1, 2).astype(BF16)
    xkl_t = jnp.swapaxes(stacked(k * g_end), 1, 2).astype(BF16)
    g_col = jnp.swapaxes(stacked(jnp.exp(cum_l), split_heads=False), 1, 2)

    ri = lax.broadcasted_iota(jnp.int32, (2 * L, 2 * L), 0)
    ci = lax.broadcasted_iota(jnp.int32, (2 * L, 2 * L), 1)
    strict, incl = ri > ci, ri >= ci
    eye = (ri == ci).astype(F32)
    pm = _bmm(jnp.concatenate([xa_, xr_], axis=1), jnp.concatenate([xb_, xk_], axis=1), _BNT)
    n_ab = jnp.where(strict, pm[:, :2 * L, :2 * L], 0.0)
    a_ak = jnp.where(strict, pm[:, :2 * L, 2 * L:], 0.0)
    a_rb = jnp.where(incl, pm[:, 2 * L:, :2 * L], 0.0).astype(BF16)
    a_rk = jnp.where(incl, pm[:, 2 * L:, 2 * L:], 0.0)
    tinv, npow = eye + n_ab, n_ab
    for _ in range(int(math.log2(L)) - 1):
        npow = _bmm(npow, npow)
        tinv = tinv + _bmm(npow, tinv)
    wu = _bmm(tinv, jnp.concatenate([xa_, _bmm(a_ak, xv_).astype(BF16)], axis=2))
    wx = jnp.concatenate([wu[:, :, :128].astype(BF16), xr_], axis=1)
    u0 = wu[:, :, 128:]
    y0 = _bmm(a_rk, xv_)
    s0 = _bmm(xkl_t, xv_)

    st = st_ref[...]
    ys = []
    for c in range(nch):
        sl = slice(c * nhp, (c + 1) * nhp)
        gs = _bmm(wx[sl], st)
        u = (gs[:, :2 * L] + u0[sl]).astype(BF16)
        yst = gs[:, 2 * L:] + _bmm(a_rb[sl], u) + y0[sl]
        st = g_col[sl] * st + _bmm(xbl_t[sl], u) + s0[sl]
        ys.append(jnp.concatenate([yst[hp, :L] + yst[hp, L:] for hp in range(nhp)], axis=1))
    st_ref[...] = st

    y = jnp.concatenate(ys, axis=0)
    inv_n = 1.0 / RW_HEAD_DIM
    mean = head_sums(y) * inv_n
    yc = y - mean
    var = head_sums(yc * yc) * inv_n
    y = yc * lax.rsqrt(var + RW_GN_EPS) * vec(_LNG) + vec(_LNB)
    o_ref[...] = (y + bonus) * g


def _rwkv(p, mu, w0, w_a, w_b, a0, a_a, a_b, g_a, g_b, k_k, k_a, r_k, ln_g, ln_b, tc=256):
    t = p.shape[0]
    w = GROUP_W
    vecs = jnp.concatenate([mu, jnp.stack([w0, a0, k_k, k_a, r_k.reshape(w), ln_g, ln_b]),
                            jnp.zeros((3, w), F32)], axis=0)
    head = jnp.arange(MXU_DIM) // RW_HEAD_DIM
    ones = (head[:, None] == head[None, :]).astype(BF16)
    chunk = jnp.arange(tc) // RW_CHUNK
    blk = chunk[:, None] == chunk[None, :]
    tril = jnp.logical_and(blk, jnp.arange(tc)[:, None] >= jnp.arange(tc)[None, :])
    const = lambda shape: pl.BlockSpec(shape, lambda i: (0,) * len(shape))
    col = lambda c: pl.BlockSpec((tc, w), lambda i: (i, c))
    bf = lambda z: z.astype(BF16)
    return pl.pallas_call(
        functools.partial(_rwkv_kernel, tc=tc),
        grid=(t // tc,),
        in_specs=[col(4), col(5), col(6), col(7), const((16, w)),
                  const(w_a.shape), const(w_b.shape), const(a_a.shape), const(a_b.shape),
                  const(g_a.shape), const(g_b.shape), const((MXU_DIM, MXU_DIM)), const((tc, tc)), const((tc, tc))],
        out_specs=pl.BlockSpec((tc, w), lambda i: (i, 0)),
        out_shape=jax.ShapeDtypeStruct((t, w), F32),
        scratch_shapes=[pltpu.VMEM((4, tc + SUBLANES, w), F32), pltpu.VMEM((w // 128, 128, 128), F32)],
        compiler_params=_cparams("arbitrary"),
        name="rwkv7",
    )(p, p, p, p, vecs, bf(w_a), bf(w_b), bf(a_a), bf(a_b), bf(g_a), bf(g_b), ones, bf(tril), bf(blk))


def _rglru_outproj_kernel(gate_ref, xr_ref, cw_ref, vec_ref, wx_ref, wa_ref,
                          x_ref, ya_ref, yb_ref, yc_ref, g_ref, w_ref, o_ref,
                          xs_ref, car_ref, *, tc, nsplit):
    first = pl.program_id(0) == 0
    halo = SUBLANES
    w = GROUP_W

    @pl.when(first)
    def _():
        xs_ref[...] = jnp.zeros_like(xs_ref)
        car_ref[...] = jnp.zeros_like(car_ref)

    xs_ref[0:halo, :] = xs_ref[tc:tc + halo, :]
    xs_ref[halo:halo + tc, :] = xr_ref[...]
    yns = [_rms(y_ref[...], g_ref[n:n + 1, :]).astype(BF16) for n, y_ref in enumerate((ya_ref, yb_ref, yc_ref))]
    lam_term = RG_C * (-_softplus(-vec_ref[3:4, :]))
    row = lax.broadcasted_iota(jnp.int32, (SUBLANES, w), 0)
    rc, cc = tc // nsplit, D_MODEL // nsplit
    carry = car_ref[...]
    hs = []
    for c in range(nsplit):
        cols = slice(c * cc, (c + 1) * cc)
        acc = x_ref[:, cols]
        for n, yn in enumerate(yns):
            acc = acc + _dot(yn, w_ref[n * w:(n + 1) * w, cols])
        o_ref[:, cols] = acc

        r0 = c * rc
        xc = vec_ref[0:1, :]
        for j in range(RG_CONV):
            off = halo + r0 - (RG_CONV - 1) + j
            xc = xc + xs_ref[off:off + rc, :] * cw_ref[j:j + 1, :]
        xcb = xc.astype(BF16)
        gate_x = _sigmoid(_dot(xcb, wx_ref[...]) + vec_ref[1:2, :])
        gate_a = _sigmoid(_dot(xcb, wa_ref[...]) + vec_ref[2:3, :])
        a = jnp.exp(lam_term * gate_a)
        mult = jnp.sqrt(1.0 - a * a)
        if c == 0:
            t_idx = lax.broadcasted_iota(jnp.int32, (rc, w), 0)
            mult = jnp.where(jnp.logical_and(first, t_idx == 0), 1.0, mult)
        b = xc * gate_x * mult
        for i in range(rc // SUBLANES):
            ab, hb = a[i * SUBLANES:(i + 1) * SUBLANES, :], b[i * SUBLANES:(i + 1) * SUBLANES, :]
            for sh in (1, 2, 4):
                ash = jnp.where(row >= sh, pltpu.roll(ab, sh, 0), 1.0)
                hsh = jnp.where(row >= sh, pltpu.roll(hb, sh, 0), 0.0)
                hb = hb + ab * hsh
                ab = ab * ash
            hb = hb + ab * jnp.broadcast_to(carry, hb.shape)
            carry = hb[SUBLANES - 1:SUBLANES, :]
            hs.append(hb)
    car_ref[...] = carry
    y_d = _gelu_tanh(gate_ref[...]) * jnp.concatenate(hs, axis=0)
    yn = _rms(y_d, g_ref[3:4, :]).astype(BF16)
    o_ref[...] += _dot(yn, w_ref[3 * w:, :])


def _rglru_outproj(p, conv_w, conv_b, w_x, b_x, w_a, b_a, lam, x, ys, g, w_out, tc=512, nsplit=8):
    t = p.shape[0]
    w = GROUP_W
    blk = jnp.arange(w) // (w // w_x.shape[0])
    same = blk[:, None] == blk[None, :]
    bdiag = lambda m: jnp.where(same, jnp.tile(m.reshape(w, -1), (1, m.shape[0])), 0.0).astype(BF16)
    vecs = jnp.concatenate([jnp.stack([conv_b, b_x, b_a, lam]), jnp.zeros((4, w), F32)], axis=0)
    const = lambda shape: pl.BlockSpec(shape, lambda i: (0,) * len(shape))
    yspec = pl.BlockSpec((tc, w), lambda i: (i, 0))
    return pl.pallas_call(
        functools.partial(_rglru_outproj_kernel, tc=tc, nsplit=nsplit),
        grid=(t // tc,),
        in_specs=[pl.BlockSpec((tc, w), lambda i: (i, 8)), pl.BlockSpec((tc, w), lambda i: (i, 9)),
                  const((RG_CONV, w)), const((8, w)), const((w, w)), const((w, w)),
                  pl.BlockSpec((tc, D_MODEL), lambda i: (i, 0)), yspec, yspec, yspec,
                  const((4, w)), const((D_MODEL, D_MODEL))],
        out_specs=pl.BlockSpec((tc, D_MODEL), lambda i: (i, 0)),
        out_shape=jax.ShapeDtypeStruct((t, D_MODEL), F32),
        scratch_shapes=[pltpu.VMEM((tc + SUBLANES, w), F32), pltpu.VMEM((1, w), F32)],
        compiler_params=_cparams("arbitrary"),
        name="rglru_outproj",
    )(p, p, conv_w, vecs, bdiag(w_x), bdiag(w_a), x, *ys, g.reshape(4, w), w_out)


def kernel(x, ffn1_norm, ffn1_w1, ffn1_w3, ffn1_w2, mix_norm, w_in, s5_lambda_re, s5_lambda_im, s5_log_dt, s5_b_re, s5_b_im, s5_c_re, s5_c_im, s5_d, s5_glu_w, s5_glu_b, sb_q_gain, sb_k_gain, rw_mu, rw_w0, rw_w_a, rw_w_b, rw_a0, rw_a_a, rw_a_b, rw_g_a, rw_g_b, rw_k_k, rw_k_a, rw_r_k, rw_ln_g, rw_ln_b, rg_conv_w, rg_conv_b, rg_w_x, rg_b_x, rg_w_a, rg_b_a, rg_lambda, out_norm, w_out, ffn2_norm, ffn2_w1, ffn2_w3, ffn2_w2):
    bn, t, _ = x.shape
    assert bn == 1
    x = x.reshape(t, D_MODEL)
    bf = lambda z: z.astype(BF16)
    for l in range(ffn1_norm.shape[0]):
        x = _ffn(x, ffn1_norm[l], ffn1_w1, ffn1_w3, ffn1_w2, l)
        p = _inproj(x, mix_norm[l], w_in, l)
        ab, bb = _s5_prep(s5_lambda_re[l], s5_lambda_im[l], s5_log_dt[l], s5_b_re[l], s5_b_im[l])
        y_a = _s5(p, ab, bb, s5_c_re[l], s5_c_im[l], s5_d[l], s5_glu_w[l], s5_glu_b[l])
        y_b = _sb(p, sb_q_gain[l], sb_k_gain[l])
        y_c = _rwkv(p, rw_mu[l], rw_w0[l], rw_w_a[l], rw_w_b[l], rw_a0[l], rw_a_a[l], rw_a_b[l],
                    rw_g_a[l], rw_g_b[l], rw_k_k[l], rw_k_a[l], rw_r_k[l], rw_ln_g[l], rw_ln_b[l])
        x = _rglru_outproj(p, rg_conv_w[l], rg_conv_b[l], rg_w_x[l], rg_b_x[l], rg_w_a[l], rg_b_a[l],
                           rg_lambda[l], x, (y_a, y_b, y_c), out_norm[l], bf(w_out[l]))
        x = _ffn(x, ffn2_norm[l], ffn2_w1, ffn2_w3, ffn2_w2, l)
    return x.reshape(bn, t, D_MODEL)
```

```python
import functools
import math

import jax
import jax.numpy as jnp
from jax import lax
from jax.experimental import pallas as pl
from jax.experimental.pallas import tpu as pltpu

F32 = jnp.float32
BF16 = jnp.bfloat16

D_MODEL = 2048
GROUP_W = 512
D_FF = 5632
D_IN = 10 * GROUP_W
S5_CH = 16
S5_GROUPS = 32
S5_STATE = 64
S5_LANES = S5_GROUPS * S5_STATE
SB_HEADS = 4
SB_HEAD_DIM = 128
RW_HEAD_DIM = 64
RW_GN_EPS = 64e-5
RG_CONV = 4
RG_C = 8.0
NORM_EPS = 1e-6
LOG2E = 1.4426950408889634

SUBLANES = 8
LANES = 128
VMEM_LIMIT = 56 * 1024 * 1024
FFN_VMEM_LIMIT = 62 * 1024 * 1024


def _cparams(*sem):
    return pltpu.CompilerParams(dimension_semantics=sem, vmem_limit_bytes=VMEM_LIMIT)


def _dot(a, b):
    return jnp.dot(a, b, preferred_element_type=F32)


def _dot_nt(a, b):
    return lax.dot_general(a, b, (((1,), (1,)), ((), ())), preferred_element_type=F32)


def _split_dot(x, m_bf16):
    hi = x.astype(BF16)
    lo = (x - hi.astype(F32)).astype(BF16)
    return _dot(hi, m_bf16) + _dot(lo, m_bf16)


def _rms(x, g):
    return x * lax.rsqrt(jnp.mean(x * x, axis=-1, keepdims=True) + NORM_EPS) * g


def _sigmoid(x):
    return 1.0 / (1.0 + jnp.exp(-x))


def _softplus(x):
    return jnp.maximum(x, 0.0) + jnp.log1p(jnp.exp(-jnp.abs(x)))


def _gelu_tanh(x):
    c = math.sqrt(2.0 / math.pi)
    return 0.5 * x * (1.0 + jnp.tanh(c * (x + 0.044715 * (x * x * x))))


def _ffn_kernel(*refs, tm, emit_next):
    if emit_next:
        (x_hbm, g_ref, gn_ref, w1_ref, w3_ref, w2_ref, o_hbm, hn_hbm,
         acc_ref, h_ref, hn_ref, ld_sem, st_sem, hn_sem) = refs
    else:
        x_hbm, g_ref, w1_ref, w3_ref, w2_ref, o_hbm, acc_ref, h_ref, ld_sem, st_sem = refs
    i, j = pl.program_id(0), pl.program_id(1)
    ni, nj = pl.num_programs(0), pl.num_programs(1)
    slot = i % 2
    rows = lambda tile: pl.ds(pl.multiple_of(tile * tm, tm), tm)
    load = lambda tile, s: pltpu.make_async_copy(x_hbm.at[rows(tile), :], acc_ref.at[s], ld_sem.at[s])
    store = lambda tile, s: pltpu.make_async_copy(acc_ref.at[s], o_hbm.at[rows(tile), :], st_sem.at[s])
    store_next = lambda tile: pltpu.make_async_copy(hn_ref, hn_hbm.at[rows(tile), :], hn_sem.at[0])

    @pl.when(jnp.logical_and(i == 0, j == 0))
    def _():
        load(0, 0).start()

    @pl.when(j == 0)
    def _():
        load(i, slot).wait()
        h_ref[...] = _rms(acc_ref[slot], g_ref[...]).astype(BF16)

    @pl.when(j == 1)
    def _():
        @pl.when(i >= 1)
        def _():
            store(i - 1, 1 - slot).wait()

        @pl.when(i + 1 < ni)
        def _():
            load(i + 1, 1 - slot).start()

    h = h_ref[...]
    a = _dot(h, w1_ref[...].astype(BF16))
    b = _dot(h, w3_ref[...].astype(BF16))
    act = (0.5 * (a * _sigmoid(a)) * b).astype(BF16)
    acc_ref[slot] += _dot(act, w2_ref[...].astype(BF16))

    @pl.when(j == nj - 1)
    def _():
        store(i, slot).start()
        if emit_next:
            @pl.when(i >= 1)
            def _():
                store_next(i - 1).wait()

            hn_ref[...] = _rms(acc_ref[slot], gn_ref[...]).astype(BF16)
            store_next(i).start()

        @pl.when(i == ni - 1)
        def _():
            store(i, slot).wait()
            if emit_next:
                store_next(i).wait()


def _ffn(x, g, w1, w3, w2, l, g_next=None, tm=1024, tf=512):
    t = x.shape[0]
    assert D_FF // tf >= 2 and t // tm >= 2
    emit = g_next is not None
    vec = pl.BlockSpec((1, D_MODEL), lambda i, j: (0, 0))
    hbm = pl.BlockSpec(memory_space=pl.ANY)
    out = pl.pallas_call(
        functools.partial(_ffn_kernel, tm=tm, emit_next=emit),
        grid=(t // tm, D_FF // tf),
        in_specs=[hbm, vec] + [vec] * emit + [
            pl.BlockSpec((None, D_MODEL, tf), lambda i, j: (l, 0, j)),
            pl.BlockSpec((None, D_MODEL, tf), lambda i, j: (l, 0, j)),
            pl.BlockSpec((None, tf, D_MODEL), lambda i, j: (l, j, 0)),
        ],
        out_specs=[hbm] + [hbm] * emit,
        out_shape=[jax.ShapeDtypeStruct((t, D_MODEL), F32)] + [jax.ShapeDtypeStruct((t, D_MODEL), BF16)] * emit,
        scratch_shapes=([pltpu.VMEM((2, tm, D_MODEL), F32), pltpu.VMEM((tm, D_MODEL), BF16)]
                        + [pltpu.VMEM((tm, D_MODEL), BF16)] * emit
                        + [pltpu.SemaphoreType.DMA((2,)), pltpu.SemaphoreType.DMA((2,))]
                        + [pltpu.SemaphoreType.DMA((1,))] * emit),
        compiler_params=pltpu.CompilerParams(dimension_semantics=("arbitrary", "arbitrary"),
                                             vmem_limit_bytes=FFN_VMEM_LIMIT),
        name="ffn",
    )(x, g.reshape(1, D_MODEL), *([g_next.reshape(1, D_MODEL)] if emit else []), w1, w3, w2)
    return tuple(out) if emit else out[0]


def _inproj_kernel(h_ref, w_ref, o_ref):
    o_ref[...] = _dot(h_ref[...], w_ref[...].astype(BF16))


def _inproj(h, w, l, tm=2048, tn=512):
    t = h.shape[0]
    return pl.pallas_call(
        _inproj_kernel,
        grid=(t // tm, D_IN // tn),
        in_specs=[
            pl.BlockSpec((tm, D_MODEL), lambda i, j: (i, 0)),
            pl.BlockSpec((None, D_MODEL, tn), lambda i, j: (l, 0, j)),
        ],
        out_specs=pl.BlockSpec((tm, tn), lambda i, j: (i, j)),
        out_shape=jax.ShapeDtypeStruct((t, D_IN), F32),
        compiler_params=_cparams("parallel", "arbitrary"),
        name="inproj",
    )(h, w)


def _outproj_kernel(x_ref, ya_ref, yb_ref, yc_ref, yd_ref, g_ref, w_ref, o_ref):
    acc = x_ref[...]
    for n, y_ref in enumerate((ya_ref, yb_ref, yc_ref, yd_ref)):
        yn = _rms(y_ref[...], g_ref[n:n + 1, :]).astype(BF16)
        acc = acc + _dot(yn, w_ref[n * GROUP_W:(n + 1) * GROUP_W, :])
    o_ref[...] = acc


def _outproj(x, ys, g, w, tm=512):
    t = x.shape[0]
    yspec = pl.BlockSpec((tm, GROUP_W), lambda i: (i, 0))
    return pl.pallas_call(
        _outproj_kernel,
        grid=(t // tm,),
        in_specs=[
            pl.BlockSpec((tm, D_MODEL), lambda i: (i, 0)),
            yspec, yspec, yspec, yspec,
            pl.BlockSpec((4, GROUP_W), lambda i: (0, 0)),
            pl.BlockSpec((D_MODEL, D_MODEL), lambda i: (0, 0)),
        ],
        out_specs=pl.BlockSpec((tm, D_MODEL), lambda i: (i, 0)),
        out_shape=jax.ShapeDtypeStruct((t, D_MODEL), F32),
        compiler_params=_cparams("parallel"),
        name="outproj",
    )(x, *ys, g.reshape(4, GROUP_W), w)


def _s5_prep_kernel(lr_ref, li_ref, ldt_ref, br_ref, bi_ref, ab_ref, bb_ref):
    lr, li = lr_ref[...], li_ref[...]
    dt = jnp.exp(ldt_ref[...])
    mag = jnp.exp(lr * dt)
    ab_re, ab_im = mag * jnp.cos(li * dt), mag * jnp.sin(li * dt)
    den = lr * lr + li * li
    g_re = ((ab_re - 1.0) * lr + ab_im * li) / den
    g_im = (ab_im * lr - (ab_re - 1.0) * li) / den
    br, bi = br_ref[...], bi_ref[...]
    ab_ref[0:1, :] = ab_re
    ab_ref[1:2, :] = ab_im
    bb_ref[0] = g_re * br - g_im * bi
    bb_ref[1] = g_re * bi + g_im * br


def _s5_prep(lam_re, lam_im, log_dt, b_re, b_im):
    row = lambda z: z.reshape(1, S5_LANES)
    ldt = jnp.broadcast_to(log_dt[:, None], (S5_GROUPS, S5_STATE))
    tr = lambda b: b.transpose(2, 0, 1).reshape(S5_CH, S5_LANES)
    return pl.pallas_call(
        _s5_prep_kernel,
        out_shape=(jax.ShapeDtypeStruct((2, S5_LANES), F32),
                   jax.ShapeDtypeStruct((2, S5_CH, S5_LANES), F32)),
        name="s5_prep",
    )(row(lam_re), row(lam_im), row(ldt), tr(b_re), tr(b_im))


def _cmul(ar, ai, br, bi):
    return ar * br - ai * bi, ar * bi + ai * br


S5_SPLIT = 4


def _s5_kernel(u_ref, ab_ref, bb_ref, cc_ref, d_ref, gw_ref, gb_ref, o_ref,
               s_ref, tab_ref, car_ref, bmat_ref, cmat_ref, *, tc, cw):
    n = S5_LANES
    nb = n // S5_SPLIT
    cb = GROUP_W // S5_SPLIT

    @pl.when(pl.program_id(0) == 0)
    def _():
        ar = jnp.broadcast_to(ab_ref[0:1, :], (SUBLANES, n))
        ai = jnp.broadcast_to(ab_ref[1:2, :], (SUBLANES, n))
        row = lax.broadcasted_iota(jnp.int32, (SUBLANES, n), 0)
        a2r, a2i = _cmul(ar, ai, ar, ai)
        a4r, a4i = _cmul(a2r, a2i, a2r, a2i)
        for k, (pr, pi, sh) in enumerate(((ar, ai, 1), (a2r, a2i, 2), (a4r, a4i, 4))):
            tab_ref[2 * k] = jnp.where(row >= sh, pr, 0.0)
            tab_ref[2 * k + 1] = jnp.where(row >= sh, pi, 0.0)
        pwr, pwi = ar, ai
        for bit, (qr, qi) in ((1, (ar, ai)), (2, (a2r, a2i)), (4, (a4r, a4i))):
            nr, ni = _cmul(pwr, pwi, qr, qi)
            sel = (row & bit) != 0
            pwr, pwi = jnp.where(sel, nr, pwr), jnp.where(sel, ni, pwi)
        tab_ref[6] = pwr
        tab_ref[7] = pwi
        car_ref[...] = jnp.zeros_like(car_ref)
        lane_grp = lax.broadcasted_iota(jnp.int32, (S5_CH, nb), 1) // S5_STATE
        for c in range(S5_SPLIT):
            st = slice(c * nb, (c + 1) * nb)
            for ref, src, sign in ((bmat_ref, bb_ref, 1.0), (cmat_ref, cc_ref, -1.0)):
                re, im = src[0, :, st], src[1, :, st] * sign
                ref[c] = jnp.concatenate(
                    [jnp.concatenate([jnp.where(lane_grp == g, re, 0.0), jnp.where(lane_grp == g, im, 0.0)], axis=1)
                     for g in range(cb // S5_CH)], axis=0).astype(BF16)

    u = u_ref[...]
    ub = u.astype(BF16)
    for c in range(S5_SPLIT):
        s_ref[:, 2 * c * nb:2 * (c + 1) * nb] = _dot(ub[:, c * cb:(c + 1) * cb], bmat_ref[c])

    cols = [(slice(c * nb + k * cw, c * nb + (k + 1) * cw),
             pl.ds(2 * c * nb + k * cw, cw), pl.ds(2 * c * nb + nb + k * cw, cw))
            for c in range(S5_SPLIT) for k in range(nb // cw)]

    def body(i, carries):
        rows = pl.ds(pl.multiple_of(i * SUBLANES, SUBLANES), SUBLANES)
        out = []
        for (st, cre, cim), (cr, ci) in zip(cols, carries):
            xr, xi = s_ref[rows, cre], s_ref[rows, cim]
            for m, sh in enumerate((1, 2, 4)):
                dr, di = _cmul(tab_ref[2 * m, :, st], tab_ref[2 * m + 1, :, st],
                               pltpu.roll(xr, sh, 0), pltpu.roll(xi, sh, 0))
                xr, xi = xr + dr, xi + di
            dr, di = _cmul(tab_ref[6, :, st], tab_ref[7, :, st],
                           jnp.broadcast_to(cr, xr.shape), jnp.broadcast_to(ci, xi.shape))
            xr, xi = xr + dr, xi + di
            s_ref[rows, cre] = xr
            s_ref[rows, cim] = xi
            out.append((xr[SUBLANES - 1:SUBLANES, :], xi[SUBLANES - 1:SUBLANES, :]))
        return tuple(out)

    carries = lax.fori_loop(0, tc // SUBLANES, body,
                            tuple((car_ref[0:1, st], car_ref[1:2, st]) for st, _, _ in cols))
    for (st, _, _), (cr, ci) in zip(cols, carries):
        car_ref[0:1, st] = cr
        car_ref[1:2, st] = ci

    y = jnp.concatenate([_dot_nt(s_ref[:, 2 * c * nb:2 * (c + 1) * nb].astype(BF16), cmat_ref[c])
                         for c in range(S5_SPLIT)], axis=1) + d_ref[...] * u
    v = _gelu_tanh(y)
    o_ref[...] = v * _sigmoid(_dot(v.astype(BF16), gw_ref[...]) + gb_ref[...])


def _s5(p, ab, bb, c_re, c_im, d, glu_w, glu_b, tc=512, cw=256):
    t = p.shape[0]
    n = S5_LANES
    cc = jnp.stack([z.transpose(1, 0, 2).reshape(S5_CH, n) for z in (c_re, c_im)])
    const = lambda shape: pl.BlockSpec(shape, lambda i: (0,) * len(shape))
    blk = (S5_SPLIT, GROUP_W // S5_SPLIT, 2 * n // S5_SPLIT)
    return pl.pallas_call(
        functools.partial(_s5_kernel, tc=tc, cw=cw),
        grid=(t // tc,),
        in_specs=[
            pl.BlockSpec((tc, GROUP_W), lambda i: (i, 0)),
            const((2, n)), const((2, S5_CH, n)), const((2, S5_CH, n)),
            const((1, GROUP_W)), const((GROUP_W, GROUP_W)), const((1, GROUP_W)),
        ],
        out_specs=pl.BlockSpec((tc, GROUP_W), lambda i: (i, 0)),
        out_shape=jax.ShapeDtypeStruct((t, GROUP_W), F32),
        scratch_shapes=[pltpu.VMEM((tc, 2 * n), F32),
                        pltpu.VMEM((8, SUBLANES, n), F32),
                        pltpu.VMEM((2, n), F32),
                        pltpu.VMEM(blk, BF16), pltpu.VMEM(blk, BF16)],
        compiler_params=_cparams("arbitrary"),
        name="s5",
    )(p, ab, bb, cc, d.reshape(1, GROUP_W), glu_w.astype(BF16), glu_b.reshape(1, GROUP_W))


def _sb_kernel(q_ref, k_ref, v_ref, qg_ref, kg_ref, tri_ref, o_ref, kn_ref, vb_ref, qn_ref, acc_ref,
               z_ref, d_ref, after_ref, w_ref, *, tq, tk, ts):
    qi = pl.program_id(1)
    nd = tq // tk
    nfull = qi * nd

    @pl.when(qi == 0)
    def _():
        kn_ref[...] = _rms(k_ref[...], kg_ref[...]).astype(BF16)
        vb_ref[...] = v_ref[...].astype(BF16)

    qn_ref[...] = (_rms(q_ref[...], qg_ref[...]) * (SB_HEAD_DIM ** -0.5)).astype(BF16)
    tri = tri_ref[...]
    diff = (lax.broadcasted_iota(jnp.int32, (ts, tk), 1) - lax.broadcasted_iota(jnp.int32, (ts, tk), 0))
    subs = tuple(range(0, tq, ts))
    krows = lambda tile: pl.ds(pl.multiple_of(tile * tk, tk), tk)

    def scores(tile):
        return _dot_nt(qn_ref[...], kn_ref[krows(tile), :])

    ntiles = nd + nfull
    hidden = -1e30

    def step(it, par, state, ds, do_q=True, do_n=True, do_w=True, do_p=True):
        cs, rs = state
        if do_q:
            z_next = scores(jnp.maximum(ntiles - 2 - it, 0))
        if do_p:
            pv = _dot(w_ref[par], vb_ref[krows(ntiles + 1 - it), :])
        if do_n:
            dvs, afters, new_rs = [], [], []
            for r0, d in zip(subs, ds):
                if d is False:
                    dvs.append(jnp.full((ts, tk), hidden, F32))
                    afters.append(jnp.zeros((ts, tk), F32))
                    new_rs.append(jnp.zeros((ts, 1), F32))
                    continue
                z = z_ref[par, r0:r0 + ts, :]
                nl = jnp.maximum(z, 0.0) + jnp.log(1.0 + jnp.exp2(jnp.abs(z) * (-LOG2E)))
                dv = z - nl
                if d is not None:
                    nl, dv = jnp.where(diff < d, nl, 0.0), jnp.where(diff < d, dv, hidden)
                dvs.append(dv)
                afters.append(_dot(nl.astype(BF16), tri))
                new_rs.append(jnp.sum(nl, axis=-1, keepdims=True))
            d_ref[par] = jnp.concatenate(dvs, axis=0)
        if do_w:
            ws = [jnp.exp(d_ref[1 - par, r0:r0 + ts, :] - after_ref[1 - par, r0:r0 + ts, :] - c).astype(BF16)
                  for r0, c in zip(subs, cs)]
            w_ref[1 - par] = jnp.concatenate(ws, axis=0)
            cs = tuple(c + r for c, r in zip(cs, rs))
        if do_n:
            after_ref[par] = jnp.concatenate(afters, axis=0)
            rs = tuple(new_rs)
        if do_q:
            z_ref[1 - par] = z_next
        if do_p:
            acc_ref[...] += pv
        return cs, rs

    z_ref[0] = scores(ntiles - 1)
    acc_ref[...] = jnp.zeros_like(acc_ref)
    zeros = tuple(jnp.zeros((ts, 1), F32) for _ in subs)
    state = (zeros, zeros)
    for it in range(nd):
        kd = (nd - 1 - it) * tk
        ds = [False if r0 - kd <= -(ts - 1) else (None if r0 - kd >= tk else r0 - kd) for r0 in subs]
        state = step(it, it % 2, state, ds, do_w=it >= 1, do_p=it >= 2)

    def two_trips(n, st):
        for par in range(2):
            st = step(nd + 2 * n + par, par, st, [None] * len(subs))
        return st

    state = lax.fori_loop(0, nfull // 2, two_trips, state)
    step(ntiles, 0, state, None, do_q=False, do_n=False)
    o_ref[...] = acc_ref[...] + _dot(w_ref[1], vb_ref[krows(0), :])


def _sb(p, q_gain, k_gain, tq=512, tk=256, ts=128):
    t = p.shape[0]
    tri = jnp.tril(jnp.ones((tk, tk), F32), -1).astype(BF16)
    return pl.pallas_call(
        functools.partial(_sb_kernel, tq=tq, tk=tk, ts=ts),
        grid=(SB_HEADS, t // tq),
        in_specs=[
            pl.BlockSpec((tq, SB_HEAD_DIM), lambda h, i: (i, 4 + h)),
            pl.BlockSpec((t, SB_HEAD_DIM), lambda h, i: (0, 8 + h)),
            pl.BlockSpec((t, SB_HEAD_DIM), lambda h, i: (0, 12 + h)),
            pl.BlockSpec((1, SB_HEAD_DIM), lambda h, i: (0, 0)),
            pl.BlockSpec((1, SB_HEAD_DIM), lambda h, i: (0, 0)),
            pl.BlockSpec((tk, tk), lambda h, i: (0, 0)),
        ],
        out_specs=pl.BlockSpec((tq, SB_HEAD_DIM), lambda h, i: (i, h)),
        out_shape=jax.ShapeDtypeStruct((t, GROUP_W), F32),
        scratch_shapes=[pltpu.VMEM((t, SB_HEAD_DIM), BF16), pltpu.VMEM((t, SB_HEAD_DIM), BF16),
                        pltpu.VMEM((tq, SB_HEAD_DIM), BF16), pltpu.VMEM((tq, SB_HEAD_DIM), F32),
                        pltpu.VMEM((2, tq, tk), F32), pltpu.VMEM((2, tq, tk), F32),
                        pltpu.VMEM((2, tq, tk), F32), pltpu.VMEM((2, tq, tk), BF16)],
        compiler_params=_cparams("arbitrary", "arbitrary"),
        name="stickbreak",
    )(p, p, p, q_gain.reshape(1, SB_HEAD_DIM), k_gain.reshape(1, SB_HEAD_DIM), tri)


RW_CHUNK = 64
(_MU_R, _MU_K, _MU_V, _MU_W, _MU_A, _MU_G, _W0, _A0, _KK, _KA, _RK, _LNG, _LNB) = range(13)


_BNN = (((2,), (1,)), ((0,), (0,)))
_BNT = (((2,), (2,)), ((0,), (0,)))


def _bmm(a, b, dims=_BNN):
    return lax.dot_general(a.astype(BF16), b.astype(BF16), dims, preferred_element_type=F32)


def _split_dot_left(m_bf16, x):
    hi = x.astype(BF16)
    lo = (x - hi.astype(F32)).astype(BF16)
    return _dot(m_bf16, hi) + _dot(m_bf16, lo)


def _rwkv_kernel(r_ref, k_ref, v_ref, m_ref, vec_ref, wa_ref, wb_ref, aa_ref, ab_ref, ga_ref, gb_ref,
                 ones_ref, tril_ref, blk_ref, o_ref, zs_ref, st_ref, *, tc):
    first = pl.program_id(0) == 0
    w = GROUP_W
    vec = lambda i: vec_ref[i:i + 1, :]

    @pl.when(first)
    def _():
        zs_ref[...] = jnp.zeros_like(zs_ref)
        st_ref[...] = jnp.zeros_like(st_ref)

    cur, prev = [], []
    for n, ref in enumerate((r_ref, k_ref, v_ref, m_ref)):
        zs_ref[n, SUBLANES - 1:SUBLANES, :] = zs_ref[n, SUBLANES + tc - 1:SUBLANES + tc, :]
        z = ref[...]
        zs_ref[n, SUBLANES:SUBLANES + tc, :] = z
        cur.append(z)
        prev.append(zs_ref[n, SUBLANES - 1:SUBLANES - 1 + tc, :])
    lerp = lambda n, mu: cur[n] + (prev[n] - cur[n]) * vec(mu)
    r, k, v = lerp(0, _MU_R), lerp(1, _MU_K), lerp(2, _MU_V)
    xw, xa, xg = lerp(3, _MU_W), lerp(3, _MU_A), lerp(3, _MU_G)

    lora = lambda x, a_ref, b_ref, f: _dot(f(_dot(x.astype(BF16), a_ref[...])).astype(BF16), b_ref[...])
    w_log = -_softplus(-(vec(_W0) + lora(xw, wa_ref, wb_ref, jnp.tanh))) - 0.5
    lw = -jnp.exp(w_log)
    a = _sigmoid(vec(_A0) + lora(xa, aa_ref, ab_ref, lambda z: z))
    g = lora(xg, ga_ref, gb_ref, _sigmoid)
    ones = ones_ref[...]
    kk = k * vec(_KK)
    kk = kk * lax.rsqrt(_split_dot(kk * kk, ones) + 1e-12)
    k = k * (1.0 + (a - 1.0) * vec(_KA))
    bonus = _split_dot(r * k * vec(_RK), ones) * v
    a_in, b_in = -kk, kk * a

    L = RW_CHUNK
    nch, nhp = tc // L, w // LANES
    cum = _split_dot_left(tril_ref[...], lw)
    cum_l = _split_dot_left(blk_ref[...], lw)
    g_inv, g_end = jnp.exp(-cum), jnp.exp(cum_l - cum)
    head0 = lax.broadcasted_iota(jnp.int32, (L, LANES), 1) < RW_HEAD_DIM

    def stacked(x, split_heads=True):
        out = []
        for c in range(nch):
            for hp in range(nhp):
                z = x[c * L:(c + 1) * L, hp * LANES:(hp + 1) * LANES]
                out.append(jnp.concatenate([jnp.where(head0, z, 0.0), jnp.where(head0, 0.0, z)], axis=0)
                           if split_heads else jnp.concatenate([z, z], axis=0))
        return jnp.stack(out)

    xa_, xr_ = stacked(a_in * jnp.exp(cum - lw)).astype(BF16), stacked(r * jnp.exp(cum)).astype(BF16)
    xb_, xk_, xv_ = stacked(b_in * g_inv).astype(BF16), stacked(k * g_inv).astype(BF16), stacked(v).astype(BF16)
    xbl_t = jnp.swapaxes(stacked(b_in * g_end), 1, 2).astype(BF16)
    xkl_t = jnp.swapaxes(stacked(k * g_end), 1, 2).astype(BF16)
    g_col = jnp.swapaxes(stacked(jnp.exp(cum_l), split_heads=False), 1, 2)

    ri = lax.broadcasted_iota(jnp.int32, (2 * L, 2 * L), 0)
    ci = lax.broadcasted_iota(jnp.int32, (2 * L, 2 * L), 1)
    strict, incl = ri > ci, ri >= ci
    eye = (ri == ci).astype(F32)
    pm = _bmm(jnp.concatenate([xa_, xr_], axis=1), jnp.concatenate([xb_, xk_], axis=1), _BNT)
    n_ab = jnp.where(strict, pm[:, :2 * L, :2 * L], 0.0)
    a_ak = jnp.where(strict, pm[:, :2 * L, 2 * L:], 0.0)
    a_rb = jnp.where(incl, pm[:, 2 * L:, :2 * L], 0.0).astype(BF16)
    a_rk = jnp.where(incl, pm[:, 2 * L:, 2 * L:], 0.0)
    tinv, npow = eye + n_ab, n_ab
    for _ in range(int(math.log2(L)) - 1):
        npow = _bmm(npow, npow)
        tinv = tinv + _bmm(npow, tinv)
    wu = _bmm(tinv, jnp.concatenate([xa_, _bmm(a_ak, xv_).astype(BF16)], axis=2))
    wx = jnp.concatenate([wu[:, :, :LANES].astype(BF16), xr_], axis=1)
    u0 = wu[:, :, LANES:]
    y0 = _bmm(a_rk, xv_)
    s0 = _bmm(xkl_t, xv_)

    st = st_ref[...]
    ys = []
    for c in range(nch):
        sl = slice(c * nhp, (c + 1) * nhp)
        gs = _bmm(wx[sl], st)
        u = (gs[:, :2 * L] + u0[sl]).astype(BF16)
        yst = gs[:, 2 * L:] + _bmm(a_rb[sl], u) + y0[sl]
        st = g_col[sl] * st + _bmm(xbl_t[sl], u) + s0[sl]
        ys.append(jnp.concatenate([yst[hp, :L] + yst[hp, L:] for hp in range(nhp)], axis=1))
    st_ref[...] = st

    y = jnp.concatenate(ys, axis=0)
    inv_n = 1.0 / RW_HEAD_DIM
    mean = _split_dot(y, ones) * inv_n
    yc = y - mean
    var = _split_dot(yc * yc, ones) * inv_n
    y = yc * lax.rsqrt(var + RW_GN_EPS) * vec(_LNG) + vec(_LNB)
    o_ref[...] = (y + bonus) * g


def _rwkv(p, mu, w0, w_a, w_b, a0, a_a, a_b, g_a, g_b, k_k, k_a, r_k, ln_g, ln_b, tc=256):
    t = p.shape[0]
    w = GROUP_W
    vecs = jnp.concatenate([mu, jnp.stack([w0, a0, k_k, k_a, r_k.reshape(w), ln_g, ln_b]),
                            jnp.zeros((3, w), F32)], axis=0)
    head = jnp.arange(w) // RW_HEAD_DIM
    ones = (head[:, None] == head[None, :]).astype(BF16)
    chunk = jnp.arange(tc) // RW_CHUNK
    blk = chunk[:, None] == chunk[None, :]
    tril = jnp.logical_and(blk, jnp.arange(tc)[:, None] >= jnp.arange(tc)[None, :])
    const = lambda shape: pl.BlockSpec(shape, lambda i: (0,) * len(shape))
    col = lambda c: pl.BlockSpec((tc, w), lambda i: (i, c))
    bf = lambda z: z.astype(BF16)
    return pl.pallas_call(
        functools.partial(_rwkv_kernel, tc=tc),
        grid=(t // tc,),
        in_specs=[col(4), col(5), col(6), col(7), const((16, w)),
                  const(w_a.shape), const(w_b.shape), const(a_a.shape), const(a_b.shape),
                  const(g_a.shape), const(g_b.shape), const((w, w)), const((tc, tc)), const((tc, tc))],
        out_specs=pl.BlockSpec((tc, w), lambda i: (i, 0)),
        out_shape=jax.ShapeDtypeStruct((t, w), F32),
        scratch_shapes=[pltpu.VMEM((4, tc + SUBLANES, w), F32), pltpu.VMEM((w // LANES, LANES, LANES), F32)],
        compiler_params=_cparams("arbitrary"),
        name="rwkv7",
    )(p, p, p, p, vecs, bf(w_a), bf(w_b), bf(a_a), bf(a_b), bf(g_a), bf(g_b), ones, bf(tril), bf(blk))


def _rglru_kernel(gate_ref, x_ref, cw_ref, vec_ref, wx_ref, wa_ref, o_ref,
                  xs_ref, a_s, b_s, car_ref, *, tc):
    first = pl.program_id(0) == 0
    halo = SUBLANES

    @pl.when(first)
    def _():
        xs_ref[...] = jnp.zeros_like(xs_ref)
        car_ref[...] = jnp.zeros_like(car_ref)

    xs_ref[0:halo, :] = xs_ref[tc:tc + halo, :]
    xs_ref[halo:halo + tc, :] = x_ref[...]
    xc = vec_ref[0:1, :]
    for j in range(RG_CONV):
        off = halo - (RG_CONV - 1) + j
        xc = xc + xs_ref[off:off + tc, :] * cw_ref[j:j + 1, :]
    xcb = xc.astype(BF16)
    gate_x = _sigmoid(_dot(xcb, wx_ref[...]) + vec_ref[1:2, :])
    gate_a = _sigmoid(_dot(xcb, wa_ref[...]) + vec_ref[2:3, :])
    lam = vec_ref[3:4, :]
    log_a = RG_C * gate_a * (-_softplus(-lam))
    a = jnp.exp(log_a)
    mult = jnp.sqrt(1.0 - a * a)
    t_idx = lax.broadcasted_iota(jnp.int32, (tc, GROUP_W), 0)
    mult = jnp.where(jnp.logical_and(first, t_idx == 0), 1.0, mult)
    a_s[...] = a
    b_s[...] = xc * gate_x * mult

    row = lax.broadcasted_iota(jnp.int32, (SUBLANES, GROUP_W), 0)

    def body(i, carry):
        rows = pl.ds(pl.multiple_of(i * SUBLANES, SUBLANES), SUBLANES)
        ab, hb = a_s[rows, :], b_s[rows, :]
        for sh in (1, 2, 4):
            ash = jnp.where(row >= sh, pltpu.roll(ab, sh, 0), 1.0)
            hsh = jnp.where(row >= sh, pltpu.roll(hb, sh, 0), 0.0)
            hb = hb + ab * hsh
            ab = ab * ash
        hb = hb + ab * jnp.broadcast_to(carry, hb.shape)
        b_s[rows, :] = hb
        return hb[SUBLANES - 1:SUBLANES, :]

    car_ref[...] = lax.fori_loop(0, tc // SUBLANES, body, car_ref[...], unroll=2)
    o_ref[...] = _gelu_tanh(gate_ref[...]) * b_s[...]


def _rglru(p, conv_w, conv_b, w_x, b_x, w_a, b_a, lam, tc=512):
    t = p.shape[0]
    w = GROUP_W
    blk = jnp.arange(w) // (w // w_x.shape[0])
    same = blk[:, None] == blk[None, :]
    bdiag = lambda m: jnp.where(same, jnp.tile(m.reshape(w, -1), (1, m.shape[0])), 0.0).astype(BF16)
    vecs = jnp.concatenate([jnp.stack([conv_b, b_x, b_a, lam]), jnp.zeros((4, w), F32)], axis=0)
    const = lambda shape: pl.BlockSpec(shape, lambda i: (0,) * len(shape))
    return pl.pallas_call(
        functools.partial(_rglru_kernel, tc=tc),
        grid=(t // tc,),
        in_specs=[pl.BlockSpec((tc, w), lambda i: (i, 8)), pl.BlockSpec((tc, w), lambda i: (i, 9)),
                  const((RG_CONV, w)), const((8, w)), const((w, w)), const((w, w))],
        out_specs=pl.BlockSpec((tc, w), lambda i: (i, 0)),
        out_shape=jax.ShapeDtypeStruct((t, w), F32),
        scratch_shapes=[pltpu.VMEM((tc + SUBLANES, w), F32), pltpu.VMEM((tc, w), F32),
                        pltpu.VMEM((tc, w), F32), pltpu.VMEM((1, w), F32)],
        compiler_params=_cparams("arbitrary"),
        name="rglru",
    )(p, p, conv_w, vecs, bdiag(w_x), bdiag(w_a))


def kernel(x, ffn1_norm, ffn1_w1, ffn1_w3, ffn1_w2, mix_norm, w_in, s5_lambda_re, s5_lambda_im, s5_log_dt, s5_b_re, s5_b_im, s5_c_re, s5_c_im, s5_d, s5_glu_w, s5_glu_b, sb_q_gain, sb_k_gain, rw_mu, rw_w0, rw_w_a, rw_w_b, rw_a0, rw_a_a, rw_a_b, rw_g_a, rw_g_b, rw_k_k, rw_k_a, rw_r_k, rw_ln_g, rw_ln_b, rg_conv_w, rg_conv_b, rg_w_x, rg_b_x, rg_w_a, rg_b_a, rg_lambda, out_norm, w_out, ffn2_norm, ffn2_w1, ffn2_w3, ffn2_w2):
    bn, t, _ = x.shape
    assert bn == 1
    x = x.reshape(t, D_MODEL)
    bf = lambda z: z.astype(BF16)
    for l in range(ffn1_norm.shape[0]):
        x, h = _ffn(x, ffn1_norm[l], ffn1_w1, ffn1_w3, ffn1_w2, l, g_next=mix_norm[l])
        p = _inproj(h, w_in, l)
        ab, bb = _s5_prep(s5_lambda_re[l], s5_lambda_im[l], s5_log_dt[l], s5_b_re[l], s5_b_im[l])
        y_a = _s5(p, ab, bb, s5_c_re[l], s5_c_im[l], s5_d[l], s5_glu_w[l], s5_glu_b[l])
        y_b = _sb(p, sb_q_gain[l], sb_k_gain[l])
        y_c = _rwkv(p, rw_mu[l], rw_w0[l], rw_w_a[l], rw_w_b[l], rw_a0[l], rw_a_a[l], rw_a_b[l],
                    rw_g_a[l], rw_g_b[l], rw_k_k[l], rw_k_a[l], rw_r_k[l], rw_ln_g[l], rw_ln_b[l])
        y_d = _rglru(p, rg_conv_w[l], rg_conv_b[l], rg_w_x[l], rg_b_x[l], rg_w_a[l], rg_b_a[l],
                     rg_lambda[l])
        x = _outproj(x, (y_a, y_b, y_c, y_d), out_norm[l], bf(w_out[l]))
        x = _ffn(x, ffn2_norm[l], ffn2_w1, ffn2_w3, ffn2_w2, l)
    return x.reshape(bn, t, D_MODEL)
```

```python
import functools
import math

import jax
import jax.numpy as jnp
from jax import lax
from jax.experimental import pallas as pl
from jax.experimental.pallas import tpu as pltpu

F32 = jnp.float32
BF16 = jnp.bfloat16

D_MODEL = 2048
GROUP_W = 512
D_FF = 5632
D_IN = 10 * GROUP_W
S5_CH = 16
S5_GROUPS = 32
S5_STATE = 64
S5_LANES = S5_GROUPS * S5_STATE
SB_HEADS = 4
SB_HEAD_DIM = 128
RW_HEAD_DIM = 64
RW_GN_EPS = 64e-5
RG_CONV = 4
RG_C = 8.0
NORM_EPS = 1e-6
LOG2E = 1.4426950408889634

SUBLANES = 8
LANES = 128
MXU_DIM = 256
VMEM_LIMIT = 56 * 1024 * 1024
FFN_VMEM_LIMIT = 62 * 1024 * 1024


def _cparams(*sem):
    return pltpu.CompilerParams(dimension_semantics=sem, vmem_limit_bytes=VMEM_LIMIT)


def _dot(a, b):
    return jnp.dot(a, b, preferred_element_type=F32)


def _dot_nt(a, b):
    return lax.dot_general(a, b, (((1,), (1,)), ((), ())), preferred_element_type=F32)


def _split_dot(x, m_bf16):
    hi = x.astype(BF16)
    lo = (x - hi.astype(F32)).astype(BF16)
    return _dot(hi, m_bf16) + _dot(lo, m_bf16)


def _rms(x, g):
    return x * lax.rsqrt(jnp.mean(x * x, axis=-1, keepdims=True) + NORM_EPS) * g


def _sigmoid(x):
    return 1.0 / (1.0 + jnp.exp(-x))


def _softplus(x):
    return jnp.maximum(x, 0.0) + jnp.log1p(jnp.exp(-jnp.abs(x)))


def _gelu_tanh(x):
    c = math.sqrt(2.0 / math.pi)
    return 0.5 * x * (1.0 + jnp.tanh(c * (x + 0.044715 * (x * x * x))))


def _ffn_kernel(*refs, tm, emit_next):
    if emit_next:
        (x_hbm, g_ref, gn_ref, w1_ref, w3_ref, w2_ref, o_hbm, hn_hbm,
         acc_ref, h_ref, hn_ref, ld_sem, st_sem, hn_sem) = refs
    else:
        x_hbm, g_ref, w1_ref, w3_ref, w2_ref, o_hbm, acc_ref, h_ref, ld_sem, st_sem = refs
    i, j = pl.program_id(0), pl.program_id(1)
    ni, nj = pl.num_programs(0), pl.num_programs(1)
    slot = i % 2
    rows = lambda tile: pl.ds(pl.multiple_of(tile * tm, tm), tm)
    load = lambda tile, s: pltpu.make_async_copy(x_hbm.at[rows(tile), :], acc_ref.at[s], ld_sem.at[s])
    store = lambda tile, s: pltpu.make_async_copy(acc_ref.at[s], o_hbm.at[rows(tile), :], st_sem.at[s])
    store_next = lambda tile: pltpu.make_async_copy(hn_ref, hn_hbm.at[rows(tile), :], hn_sem.at[0])

    @pl.when(jnp.logical_and(i == 0, j == 0))
    def _():
        load(0, 0).start()

    @pl.when(j == 0)
    def _():
        load(i, slot).wait()
        h_ref[...] = _rms(acc_ref[slot], g_ref[...]).astype(BF16)

    @pl.when(j == 1)
    def _():
        @pl.when(i >= 1)
        def _():
            store(i - 1, 1 - slot).wait()

        @pl.when(i + 1 < ni)
        def _():
            load(i + 1, 1 - slot).start()

    h = h_ref[...]
    a = _dot(h, w1_ref[...].astype(BF16))
    b = _dot(h, w3_ref[...].astype(BF16))
    act = (0.5 * (a * _sigmoid(a)) * b).astype(BF16)
    acc_ref[slot] += _dot(act, w2_ref[...].astype(BF16))

    @pl.when(j == nj - 1)
    def _():
        store(i, slot).start()
        if emit_next:
            @pl.when(i >= 1)
            def _():
                store_next(i - 1).wait()

            hn_ref[...] = _rms(acc_ref[slot], gn_ref[...]).astype(BF16)
            store_next(i).start()

        @pl.when(i == ni - 1)
        def _():
            store(i, slot).wait()
            if emit_next:
                store_next(i).wait()


def _ffn(x, g, w1, w3, w2, l, g_next=None, tm=1024, tf=512):
    t = x.shape[0]
    assert D_FF // tf >= 2 and t // tm >= 2
    emit = g_next is not None
    vec = pl.BlockSpec((1, D_MODEL), lambda i, j: (0, 0))
    hbm = pl.BlockSpec(memory_space=pl.ANY)
    out = pl.pallas_call(
        functools.partial(_ffn_kernel, tm=tm, emit_next=emit),
        grid=(t // tm, D_FF // tf),
        in_specs=[hbm, vec] + [vec] * emit + [
            pl.BlockSpec((None, D_MODEL, tf), lambda i, j: (l, 0, j)),
            pl.BlockSpec((None, D_MODEL, tf), lambda i, j: (l, 0, j)),
            pl.BlockSpec((None, tf, D_MODEL), lambda i, j: (l, j, 0)),
        ],
        out_specs=[hbm] + [hbm] * emit,
        out_shape=[jax.ShapeDtypeStruct((t, D_MODEL), F32)] + [jax.ShapeDtypeStruct((t, D_MODEL), BF16)] * emit,
        scratch_shapes=([pltpu.VMEM((2, tm, D_MODEL), F32), pltpu.VMEM((tm, D_MODEL), BF16)]
                        + [pltpu.VMEM((tm, D_MODEL), BF16)] * emit
                        + [pltpu.SemaphoreType.DMA((2,)), pltpu.SemaphoreType.DMA((2,))]
                        + [pltpu.SemaphoreType.DMA((1,))] * emit),
        compiler_params=pltpu.CompilerParams(dimension_semantics=("arbitrary", "arbitrary"),
                                             vmem_limit_bytes=FFN_VMEM_LIMIT),
        name="ffn",
    )(x, g.reshape(1, D_MODEL), *([g_next.reshape(1, D_MODEL)] if emit else []), w1, w3, w2)
    return tuple(out) if emit else out[0]


def _inproj_kernel(h_ref, w_ref, o_ref):
    o_ref[...] = _dot(h_ref[...], w_ref[...].astype(BF16))


def _inproj(h, w, l, tm=2048, tn=512):
    t = h.shape[0]
    return pl.pallas_call(
        _inproj_kernel,
        grid=(t // tm, D_IN // tn),
        in_specs=[
            pl.BlockSpec((tm, D_MODEL), lambda i, j: (i, 0)),
            pl.BlockSpec((None, D_MODEL, tn), lambda i, j: (l, 0, j)),
        ],
        out_specs=pl.BlockSpec((tm, tn), lambda i, j: (i, j)),
        out_shape=jax.ShapeDtypeStruct((t, D_IN), F32),
        compiler_params=_cparams("parallel", "arbitrary"),
        name="inproj",
    )(h, w)


def _outproj_kernel(x_ref, ya_ref, yb_ref, yc_ref, yd_ref, g_ref, w_ref, o_ref):
    acc = x_ref[...]
    for n, y_ref in enumerate((ya_ref, yb_ref, yc_ref, yd_ref)):
        yn = _rms(y_ref[...], g_ref[n:n + 1, :]).astype(BF16)
        acc = acc + _dot(yn, w_ref[n * GROUP_W:(n + 1) * GROUP_W, :].astype(BF16))
    o_ref[...] = acc


def _outproj(x, ys, g, w, l, tm=512):
    t = x.shape[0]
    yspec = pl.BlockSpec((tm, GROUP_W), lambda i: (i, 0))
    return pl.pallas_call(
        _outproj_kernel,
        grid=(t // tm,),
        in_specs=[
            pl.BlockSpec((tm, D_MODEL), lambda i: (i, 0)),
            yspec, yspec, yspec, yspec,
            pl.BlockSpec((4, GROUP_W), lambda i: (0, 0)),
            pl.BlockSpec((None, D_MODEL, D_MODEL), lambda i: (l, 0, 0), pipeline_mode=pl.Buffered(1)),
        ],
        out_specs=pl.BlockSpec((tm, D_MODEL), lambda i: (i, 0)),
        out_shape=jax.ShapeDtypeStruct((t, D_MODEL), F32),
        compiler_params=_cparams("parallel"),
        name="outproj",
    )(x, *ys, g.reshape(4, GROUP_W), w)


def _s5_prep_kernel(lr_ref, li_ref, ldt_ref, br_ref, bi_ref, ab_ref, bb_ref):
    lr, li = lr_ref[...], li_ref[...]
    dt = jnp.exp(ldt_ref[...])
    mag = jnp.exp(lr * dt)
    ab_re, ab_im = mag * jnp.cos(li * dt), mag * jnp.sin(li * dt)
    den = lr * lr + li * li
    g_re = ((ab_re - 1.0) * lr + ab_im * li) / den
    g_im = (ab_im * lr - (ab_re - 1.0) * li) / den
    br, bi = br_ref[...], bi_ref[...]
    ab_ref[0:1, :] = ab_re
    ab_ref[1:2, :] = ab_im
    bb_ref[0] = g_re * br - g_im * bi
    bb_ref[1] = g_re * bi + g_im * br


def _s5_prep(lam_re, lam_im, log_dt, b_re, b_im):
    row = lambda z: z.reshape(1, S5_LANES)
    ldt = jnp.broadcast_to(log_dt[:, None], (S5_GROUPS, S5_STATE))
    tr = lambda b: b.transpose(2, 0, 1).reshape(S5_CH, S5_LANES)
    return pl.pallas_call(
        _s5_prep_kernel,
        out_shape=(jax.ShapeDtypeStruct((2, S5_LANES), F32),
                   jax.ShapeDtypeStruct((2, S5_CH, S5_LANES), F32)),
        name="s5_prep",
    )(row(lam_re), row(lam_im), row(ldt), tr(b_re), tr(b_im))


def _cmul(ar, ai, br, bi):
    return ar * br - ai * bi, ar * bi + ai * br


S5_SPLIT = 4


def _s5_kernel(u_ref, ab_ref, bb_ref, cc_ref, d_ref, gw_ref, gb_ref, o_ref,
               s_ref, tab_ref, car_ref, bmat_ref, cmat_ref, *, tc, cw):
    n = S5_LANES
    nb = n // S5_SPLIT
    cb = GROUP_W // S5_SPLIT

    @pl.when(pl.program_id(0) == 0)
    def _():
        ar = jnp.broadcast_to(ab_ref[0:1, :], (SUBLANES, n))
        ai = jnp.broadcast_to(ab_ref[1:2, :], (SUBLANES, n))
        row = lax.broadcasted_iota(jnp.int32, (SUBLANES, n), 0)
        a2r, a2i = _cmul(ar, ai, ar, ai)
        a4r, a4i = _cmul(a2r, a2i, a2r, a2i)
        for k, (pr, pi, sh) in enumerate(((ar, ai, 1), (a2r, a2i, 2), (a4r, a4i, 4))):
            tab_ref[2 * k] = jnp.where(row >= sh, pr, 0.0)
            tab_ref[2 * k + 1] = jnp.where(row >= sh, pi, 0.0)
        pwr, pwi = ar, ai
        for bit, (qr, qi) in ((1, (ar, ai)), (2, (a2r, a2i)), (4, (a4r, a4i))):
            nr, ni = _cmul(pwr, pwi, qr, qi)
            sel = (row & bit) != 0
            pwr, pwi = jnp.where(sel, nr, pwr), jnp.where(sel, ni, pwi)
        tab_ref[6] = pwr
        tab_ref[7] = pwi
        car_ref[...] = jnp.zeros_like(car_ref)
        lane_grp = lax.broadcasted_iota(jnp.int32, (S5_CH, nb), 1) // S5_STATE
        for c in range(S5_SPLIT):
            st = slice(c * nb, (c + 1) * nb)
            for ref, src, sign in ((bmat_ref, bb_ref, 1.0), (cmat_ref, cc_ref, -1.0)):
                re, im = src[0, :, st], src[1, :, st] * sign
                ref[c] = jnp.concatenate(
                    [jnp.concatenate([jnp.where(lane_grp == g, re, 0.0), jnp.where(lane_grp == g, im, 0.0)], axis=1)
                     for g in range(cb // S5_CH)], axis=0).astype(BF16)

    u = u_ref[...]
    ub = u.astype(BF16)
    for c in range(S5_SPLIT):
        s_ref[:, 2 * c * nb:2 * (c + 1) * nb] = _dot(ub[:, c * cb:(c + 1) * cb], bmat_ref[c])

    cols = [(slice(c * nb + k * cw, c * nb + (k + 1) * cw),
             pl.ds(2 * c * nb + k * cw, cw), pl.ds(2 * c * nb + nb + k * cw, cw))
            for c in range(S5_SPLIT) for k in range(nb // cw)]

    def body(i, carries):
        rows = pl.ds(pl.multiple_of(i * SUBLANES, SUBLANES), SUBLANES)
        out = []
        for (st, cre, cim), (cr, ci) in zip(cols, carries):
            xr, xi = s_ref[rows, cre], s_ref[rows, cim]
            for m, sh in enumerate((1, 2, 4)):
                dr, di = _cmul(tab_ref[2 * m, :, st], tab_ref[2 * m + 1, :, st],
                               pltpu.roll(xr, sh, 0), pltpu.roll(xi, sh, 0))
                xr, xi = xr + dr, xi + di
            dr, di = _cmul(tab_ref[6, :, st], tab_ref[7, :, st],
                           jnp.broadcast_to(cr, xr.shape), jnp.broadcast_to(ci, xi.shape))
            xr, xi = xr + dr, xi + di
            s_ref[rows, cre] = xr
            s_ref[rows, cim] = xi
            out.append((xr[SUBLANES - 1:SUBLANES, :], xi[SUBLANES - 1:SUBLANES, :]))
        return tuple(out)

    carries = lax.fori_loop(0, tc // SUBLANES, body,
                            tuple((car_ref[0:1, st], car_ref[1:2, st]) for st, _, _ in cols))
    for (st, _, _), (cr, ci) in zip(cols, carries):
        car_ref[0:1, st] = cr
        car_ref[1:2, st] = ci

    y = jnp.concatenate([_dot_nt(s_ref[:, 2 * c * nb:2 * (c + 1) * nb].astype(BF16), cmat_ref[c])
                         for c in range(S5_SPLIT)], axis=1) + d_ref[...] * u
    v = _gelu_tanh(y)
    o_ref[...] = v * _sigmoid(_dot(v.astype(BF16), gw_ref[...]) + gb_ref[...])


def _s5(p, ab, bb, c_re, c_im, d, glu_w, glu_b, tc=512, cw=256):
    t = p.shape[0]
    n = S5_LANES
    cc = jnp.stack([z.transpose(1, 0, 2).reshape(S5_CH, n) for z in (c_re, c_im)])
    const = lambda shape: pl.BlockSpec(shape, lambda i: (0,) * len(shape))
    blk = (S5_SPLIT, GROUP_W // S5_SPLIT, 2 * n // S5_SPLIT)
    return pl.pallas_call(
        functools.partial(_s5_kernel, tc=tc, cw=cw),
        grid=(t // tc,),
        in_specs=[
            pl.BlockSpec((tc, GROUP_W), lambda i: (i, 0)),
            const((2, n)), const((2, S5_CH, n)), const((2, S5_CH, n)),
            const((1, GROUP_W)), const((GROUP_W, GROUP_W)), const((1, GROUP_W)),
        ],
        out_specs=pl.BlockSpec((tc, GROUP_W), lambda i: (i, 0)),
        out_shape=jax.ShapeDtypeStruct((t, GROUP_W), F32),
        scratch_shapes=[pltpu.VMEM((tc, 2 * n), F32),
                        pltpu.VMEM((8, SUBLANES, n), F32),
                        pltpu.VMEM((2, n), F32),
                        pltpu.VMEM(blk, BF16), pltpu.VMEM(blk, BF16)],
        compiler_params=_cparams("arbitrary"),
        name="s5",
    )(p, ab, bb, cc, d.reshape(1, GROUP_W), glu_w.astype(BF16), glu_b.reshape(1, GROUP_W))


def _sb_kernel(q_ref, k_ref, v_ref, qg_ref, kg_ref, tri_ref, o_ref, kn_ref, vb_ref, qn_ref, acc_ref,
               z_ref, d_ref, after_ref, w_ref, *, tq, tk, ts):
    qi = pl.program_id(1)
    nd = tq // tk
    nfull = qi * nd

    @pl.when(qi == 0)
    def _():
        kn_ref[...] = _rms(k_ref[...], kg_ref[...]).astype(BF16)
        vb_ref[...] = v_ref[...].astype(BF16)

    qn_ref[...] = (_rms(q_ref[...], qg_ref[...]) * (SB_HEAD_DIM ** -0.5)).astype(BF16)
    tri = tri_ref[...]
    diff = (lax.broadcasted_iota(jnp.int32, (ts, tk), 1) - lax.broadcasted_iota(jnp.int32, (ts, tk), 0))
    subs = tuple(range(0, tq, ts))
    krows = lambda tile: pl.ds(pl.multiple_of(tile * tk, tk), tk)

    def scores(tile):
        return _dot_nt(qn_ref[...], kn_ref[krows(tile), :])

    ntiles = nd + nfull
    hidden = -1e30

    def step(it, par, state, ds, do_q=True, do_n=True, do_w=True, do_p=True):
        cs, rs = state
        if do_q:
            z_next = scores(jnp.maximum(ntiles - 2 - it, 0))
        if do_p:
            pv = _dot(w_ref[par], vb_ref[krows(ntiles + 1 - it), :])
        if do_n:
            dvs, afters, new_rs = [], [], []
            for r0, d in zip(subs, ds):
                if d is False:
                    dvs.append(jnp.full((ts, tk), hidden, F32))
                    afters.append(jnp.zeros((ts, tk), F32))
                    new_rs.append(jnp.zeros((ts, 1), F32))
                    continue
                z = z_ref[par, r0:r0 + ts, :]
                nl = jnp.maximum(z, 0.0) + jnp.log(1.0 + jnp.exp2(jnp.abs(z) * (-LOG2E)))
                dv = z - nl
                if d is not None:
                    nl, dv = jnp.where(diff < d, nl, 0.0), jnp.where(diff < d, dv, hidden)
                dvs.append(dv)
                afters.append(_dot(nl.astype(BF16), tri))
                new_rs.append(jnp.sum(nl, axis=-1, keepdims=True))
            d_ref[par] = jnp.concatenate(dvs, axis=0)
        if do_w:
            ws = [jnp.exp(d_ref[1 - par, r0:r0 + ts, :] - after_ref[1 - par, r0:r0 + ts, :] - c).astype(BF16)
                  for r0, c in zip(subs, cs)]
            w_ref[1 - par] = jnp.concatenate(ws, axis=0)
            cs = tuple(c + r for c, r in zip(cs, rs))
        if do_n:
            after_ref[par] = jnp.concatenate(afters, axis=0)
            rs = tuple(new_rs)
        if do_q:
            z_ref[1 - par] = z_next
        if do_p:
            acc_ref[...] += pv
        return cs, rs

    z_ref[0] = scores(ntiles - 1)
    acc_ref[...] = jnp.zeros_like(acc_ref)
    zeros = tuple(jnp.zeros((ts, 1), F32) for _ in subs)
    state = (zeros, zeros)
    for it in range(nd):
        kd = (nd - 1 - it) * tk
        ds = [False if r0 - kd <= -(ts - 1) else (None if r0 - kd >= tk else r0 - kd) for r0 in subs]
        state = step(it, it % 2, state, ds, do_w=it >= 1, do_p=it >= 2)

    def two_trips(n, st):
        for par in range(2):
            st = step(nd + 2 * n + par, par, st, [None] * len(subs))
        return st

    state = lax.fori_loop(0, nfull // 2, two_trips, state)
    step(ntiles, 0, state, None, do_q=False, do_n=False)
    o_ref[...] = acc_ref[...] + _dot(w_ref[1], vb_ref[krows(0), :])


def _sb(p, q_gain, k_gain, tq=512, tk=256, ts=128):
    t = p.shape[0]
    tri = jnp.tril(jnp.ones((tk, tk), F32), -1).astype(BF16)
    return pl.pallas_call(
        functools.partial(_sb_kernel, tq=tq, tk=tk, ts=ts),
        grid=(SB_HEADS, t // tq),
        in_specs=[
            pl.BlockSpec((tq, SB_HEAD_DIM), lambda h, i: (i, 4 + h)),
            pl.BlockSpec((t, SB_HEAD_DIM), lambda h, i: (0, 8 + h)),
            pl.BlockSpec((t, SB_HEAD_DIM), lambda h, i: (0, 12 + h)),
            pl.BlockSpec((1, SB_HEAD_DIM), lambda h, i: (0, 0)),
            pl.BlockSpec((1, SB_HEAD_DIM), lambda h, i: (0, 0)),
            pl.BlockSpec((tk, tk), lambda h, i: (0, 0)),
        ],
        out_specs=pl.BlockSpec((tq, SB_HEAD_DIM), lambda h, i: (i, h)),
        out_shape=jax.ShapeDtypeStruct((t, GROUP_W), F32),
        scratch_shapes=[pltpu.VMEM((t, SB_HEAD_DIM), BF16), pltpu.VMEM((t, SB_HEAD_DIM), BF16),
                        pltpu.VMEM((tq, SB_HEAD_DIM), BF16), pltpu.VMEM((tq, SB_HEAD_DIM), F32),
                        pltpu.VMEM((2, tq, tk), F32), pltpu.VMEM((2, tq, tk), F32),
                        pltpu.VMEM((2, tq, tk), F32), pltpu.VMEM((2, tq, tk), BF16)],
        compiler_params=_cparams("arbitrary", "arbitrary"),
        name="stickbreak",
    )(p, p, p, q_gain.reshape(1, SB_HEAD_DIM), k_gain.reshape(1, SB_HEAD_DIM), tri)


RW_CHUNK = 64
(_MU_R, _MU_K, _MU_V, _MU_W, _MU_A, _MU_G, _W0, _A0, _KK, _KA, _RK, _LNG, _LNB) = range(13)


_BNN = (((2,), (1,)), ((0,), (0,)))
_BNT = (((2,), (2,)), ((0,), (0,)))


def _bmm(a, b, dims=_BNN):
    return lax.dot_general(a.astype(BF16), b.astype(BF16), dims, preferred_element_type=F32)


def _split_dot_left(m_bf16, x):
    hi = x.astype(BF16)
    lo = (x - hi.astype(F32)).astype(BF16)
    return _dot(m_bf16, hi) + _dot(m_bf16, lo)


def _rwkv_kernel(r_ref, k_ref, v_ref, m_ref, vec_ref, wa_ref, wb_ref, aa_ref, ab_ref, ga_ref, gb_ref,
                 ones_ref, tril_ref, blk_ref, o_ref, zs_ref, st_ref, *, tc):
    first = pl.program_id(0) == 0
    w = GROUP_W
    vec = lambda i: vec_ref[i:i + 1, :]

    @pl.when(first)
    def _():
        zs_ref[...] = jnp.zeros_like(zs_ref)
        st_ref[...] = jnp.zeros_like(st_ref)

    cur, prev = [], []
    for n, ref in enumerate((r_ref, k_ref, v_ref, m_ref)):
        zs_ref[n, SUBLANES - 1:SUBLANES, :] = zs_ref[n, SUBLANES + tc - 1:SUBLANES + tc, :]
        z = ref[...]
        zs_ref[n, SUBLANES:SUBLANES + tc, :] = z
        cur.append(z)
        prev.append(zs_ref[n, SUBLANES - 1:SUBLANES - 1 + tc, :])
    lerp = lambda n, mu: cur[n] + (prev[n] - cur[n]) * vec(mu)
    r, k, v = lerp(0, _MU_R), lerp(1, _MU_K), lerp(2, _MU_V)
    xw, xa, xg = lerp(3, _MU_W), lerp(3, _MU_A), lerp(3, _MU_G)

    lora = lambda x, a_ref, b_ref, f: _dot(f(_dot(x.astype(BF16), a_ref[...])).astype(BF16), b_ref[...])
    w_log = -_softplus(-(vec(_W0) + lora(xw, wa_ref, wb_ref, jnp.tanh))) - 0.5
    lw = -jnp.exp(w_log)
    a = _sigmoid(vec(_A0) + lora(xa, aa_ref, ab_ref, lambda z: z))
    g = lora(xg, ga_ref, gb_ref, _sigmoid)
    ones = ones_ref[...]

    def head_sums(x):
        return jnp.concatenate([_split_dot(x[:, c:c + MXU_DIM], ones) for c in range(0, w, MXU_DIM)], axis=1)

    kk = k * vec(_KK)
    kk = kk * lax.rsqrt(head_sums(kk * kk) + 1e-12)
    k = k * (1.0 + (a - 1.0) * vec(_KA))
    bonus = head_sums(r * k * vec(_RK)) * v
    a_in, b_in = -kk, kk * a

    L = RW_CHUNK
    nch, nhp = tc // L, w // LANES
    cum = _split_dot_left(tril_ref[...], lw)
    cum_l = _split_dot_left(blk_ref[...], lw)
    g_inv, g_end = jnp.exp(-cum), jnp.exp(cum_l - cum)
    head0 = lax.broadcasted_iota(jnp.int32, (L, LANES), 1) < RW_HEAD_DIM

    def stacked(x, split_heads=True):
        out = []
        for c in range(nch):
            for hp in range(nhp):
                z = x[c * L:(c + 1) * L, hp * LANES:(hp + 1) * LANES]
                out.append(jnp.concatenate([jnp.where(head0, z, 0.0), jnp.where(head0, 0.0, z)], axis=0)
                           if split_heads else jnp.concatenate([z, z], axis=0))
        return jnp.stack(out)

    xa_, xr_ = stacked(a_in * jnp.exp(cum - lw)).astype(BF16), stacked(r * jnp.exp(cum)).astype(BF16)
    xb_, xk_, xv_ = stacked(b_in * g_inv).astype(BF16), stacked(k * g_inv).astype(BF16), stacked(v).astype(BF16)
    xbl_t = jnp.swapaxes(stacked(b_in * g_end), 1, 2).astype(BF16)
    xkl_t = jnp.swapaxes(stacked(k * g_end), 1, 2).astype(BF16)
    g_col = jnp.swapaxes(stacked(jnp.exp(cum_l), split_heads=False), 1, 2)

    ri = lax.broadcasted_iota(jnp.int32, (2 * L, 2 * L), 0)
    ci = lax.broadcasted_iota(jnp.int32, (2 * L, 2 * L), 1)
    strict, incl = ri > ci, ri >= ci
    eye = (ri == ci).astype(F32)
    pm = _bmm(jnp.concatenate([xa_, xr_], axis=1), jnp.concatenate([xb_, xk_], axis=1), _BNT)
    n_ab = jnp.where(strict, pm[:, :2 * L, :2 * L], 0.0)
    a_ak = jnp.where(strict, pm[:, :2 * L, 2 * L:], 0.0)
    a_rb = jnp.where(incl, pm[:, 2 * L:, :2 * L], 0.0).astype(BF16)
    a_rk = jnp.where(incl, pm[:, 2 * L:, 2 * L:], 0.0)
    tinv, npow = eye + n_ab, n_ab
    for _ in range(int(math.log2(L)) - 1):
        npow = _bmm(npow, npow)
        tinv = tinv + _bmm(npow, tinv)
    wu = _bmm(tinv, jnp.concatenate([xa_, _bmm(a_ak, xv_).astype(BF16)], axis=2))
    wx = jnp.concatenate([wu[:, :, :LANES].astype(BF16), xr_], axis=1)
    u0 = wu[:, :, LANES:]
    y0 = _bmm(a_rk, xv_)
    s0 = _bmm(xkl_t, xv_)

    st = st_ref[...]
    ys = []
    for c in range(nch):
        sl = slice(c * nhp, (c + 1) * nhp)
        gs = _bmm(wx[sl], st)
        u = (gs[:, :2 * L] + u0[sl]).astype(BF16)
        yst = gs[:, 2 * L:] + _bmm(a_rb[sl], u) + y0[sl]
        st = g_col[sl] * st + _bmm(xbl_t[sl], u) + s0[sl]
        ys.append(jnp.concatenate([yst[hp, :L] + yst[hp, L:] for hp in range(nhp)], axis=1))
    st_ref[...] = st

    y = jnp.concatenate(ys, axis=0)
    inv_n = 1.0 / RW_HEAD_DIM
    mean = head_sums(y) * inv_n
    yc = y - mean
    var = head_sums(yc * yc) * inv_n
    y = yc * lax.rsqrt(var + RW_GN_EPS) * vec(_LNG) + vec(_LNB)
    o_ref[...] = (y + bonus) * g


def _rwkv(p, mu, w0, w_a, w_b, a0, a_a, a_b, g_a, g_b, k_k, k_a, r_k, ln_g, ln_b, tc=256):
    t = p.shape[0]
    w = GROUP_W
    vecs = jnp.concatenate([mu, jnp.stack([w0, a0, k_k, k_a, r_k.reshape(w), ln_g, ln_b]),
                            jnp.zeros((3, w), F32)], axis=0)
    head = jnp.arange(MXU_DIM) // RW_HEAD_DIM
    ones = (head[:, None] == head[None, :]).astype(BF16)
    chunk = jnp.arange(tc) // RW_CHUNK
    blk = chunk[:, None] == chunk[None, :]
    tril = jnp.logical_and(blk, jnp.arange(tc)[:, None] >= jnp.arange(tc)[None, :])
    const = lambda shape: pl.BlockSpec(shape, lambda i: (0,) * len(shape))
    col = lambda c: pl.BlockSpec((tc, w), lambda i: (i, c))
    bf = lambda z: z.astype(BF16)
    return pl.pallas_call(
        functools.partial(_rwkv_kernel, tc=tc),
        grid=(t // tc,),
        in_specs=[col(4), col(5), col(6), col(7), const((16, w)),
                  const(w_a.shape), const(w_b.shape), const(a_a.shape), const(a_b.shape),
                  const(g_a.shape), const(g_b.shape), const((MXU_DIM, MXU_DIM)), const((tc, tc)), const((tc, tc))],
        out_specs=pl.BlockSpec((tc, w), lambda i: (i, 0)),
        out_shape=jax.ShapeDtypeStruct((t, w), F32),
        scratch_shapes=[pltpu.VMEM((4, tc + SUBLANES, w), F32), pltpu.VMEM((w // LANES, LANES, LANES), F32)],
        compiler_params=_cparams("arbitrary"),
        name="rwkv7",
    )(p, p, p, p, vecs, bf(w_a), bf(w_b), bf(a_a), bf(a_b), bf(g_a), bf(g_b), ones, bf(tril), bf(blk))


def _rglru_kernel(gate_ref, x_ref, cw_ref, vec_ref, wx_ref, wa_ref, o_ref,
                  xs_ref, a_s, b_s, car_ref, *, tc):
    first = pl.program_id(0) == 0
    halo = SUBLANES

    @pl.when(first)
    def _():
        xs_ref[...] = jnp.zeros_like(xs_ref)
        car_ref[...] = jnp.zeros_like(car_ref)

    xs_ref[0:halo, :] = xs_ref[tc:tc + halo, :]
    xs_ref[halo:halo + tc, :] = x_ref[...]
    xc = vec_ref[0:1, :]
    for j in range(RG_CONV):
        off = halo - (RG_CONV - 1) + j
        xc = xc + xs_ref[off:off + tc, :] * cw_ref[j:j + 1, :]
    xcb = xc.astype(BF16)
    gate_x = _sigmoid(_dot(xcb, wx_ref[...]) + vec_ref[1:2, :])
    gate_a = _sigmoid(_dot(xcb, wa_ref[...]) + vec_ref[2:3, :])
    lam = vec_ref[3:4, :]
    log_a = RG_C * gate_a * (-_softplus(-lam))
    a = jnp.exp(log_a)
    mult = jnp.sqrt(1.0 - a * a)
    t_idx = lax.broadcasted_iota(jnp.int32, (tc, GROUP_W), 0)
    mult = jnp.where(jnp.logical_and(first, t_idx == 0), 1.0, mult)
    a_s[...] = a
    b_s[...] = xc * gate_x * mult

    row = lax.broadcasted_iota(jnp.int32, (SUBLANES, GROUP_W), 0)

    def body(i, carry):
        rows = pl.ds(pl.multiple_of(i * SUBLANES, SUBLANES), SUBLANES)
        ab, hb = a_s[rows, :], b_s[rows, :]
        for sh in (1, 2, 4):
            ash = jnp.where(row >= sh, pltpu.roll(ab, sh, 0), 1.0)
            hsh = jnp.where(row >= sh, pltpu.roll(hb, sh, 0), 0.0)
            hb = hb + ab * hsh
            ab = ab * ash
        hb = hb + ab * jnp.broadcast_to(carry, hb.shape)
        b_s[rows, :] = hb
        return hb[SUBLANES - 1:SUBLANES, :]

    car_ref[...] = lax.fori_loop(0, tc // SUBLANES, body, car_ref[...], unroll=2)
    o_ref[...] = _gelu_tanh(gate_ref[...]) * b_s[...]


def _rglru(p, conv_w, conv_b, w_x, b_x, w_a, b_a, lam, tc=512):
    t = p.shape[0]
    w = GROUP_W
    blk = jnp.arange(w) // (w // w_x.shape[0])
    same = blk[:, None] == blk[None, :]
    bdiag = lambda m: jnp.where(same, jnp.tile(m.reshape(w, -1), (1, m.shape[0])), 0.0).astype(BF16)
    vecs = jnp.concatenate([jnp.stack([conv_b, b_x, b_a, lam]), jnp.zeros((4, w), F32)], axis=0)
    const = lambda shape: pl.BlockSpec(shape, lambda i: (0,) * len(shape))
    return pl.pallas_call(
        functools.partial(_rglru_kernel, tc=tc),
        grid=(t // tc,),
        in_specs=[pl.BlockSpec((tc, w), lambda i: (i, 8)), pl.BlockSpec((tc, w), lambda i: (i, 9)),
                  const((RG_CONV, w)), const((8, w)), const((w, w)), const((w, w))],
        out_specs=pl.BlockSpec((tc, w), lambda i: (i, 0)),
        out_shape=jax.ShapeDtypeStruct((t, w), F32),
        scratch_shapes=[pltpu.VMEM((tc + SUBLANES, w), F32), pltpu.VMEM((tc, w), F32),
                        pltpu.VMEM((tc, w), F32), pltpu.VMEM((1, w), F32)],
        compiler_params=_cparams("arbitrary"),
        name="rglru",
    )(p, p, conv_w, vecs, bdiag(w_x), bdiag(w_a))


def kernel(x, ffn1_norm, ffn1_w1, ffn1_w3, ffn1_w2, mix_norm, w_in, s5_lambda_re, s5_lambda_im, s5_log_dt, s5_b_re, s5_b_im, s5_c_re, s5_c_im, s5_d, s5_glu_w, s5_glu_b, sb_q_gain, sb_k_gain, rw_mu, rw_w0, rw_w_a, rw_w_b, rw_a0, rw_a_a, rw_a_b, rw_g_a, rw_g_b, rw_k_k, rw_k_a, rw_r_k, rw_ln_g, rw_ln_b, rg_conv_w, rg_conv_b, rg_w_x, rg_b_x, rg_w_a, rg_b_a, rg_lambda, out_norm, w_out, ffn2_norm, ffn2_w1, ffn2_w3, ffn2_w2):
    bn, t, _ = x.shape
    assert bn == 1
    x = x.reshape(t, D_MODEL)
    for l in range(ffn1_norm.shape[0]):
        x, h = _ffn(x, ffn1_norm[l], ffn1_w1, ffn1_w3, ffn1_w2, l, g_next=mix_norm[l])
        p = _inproj(h, w_in, l)
        ab, bb = _s5_prep(s5_lambda_re[l], s5_lambda_im[l], s5_log_dt[l], s5_b_re[l], s5_b_im[l])
        y_a = _s5(p, ab, bb, s5_c_re[l], s5_c_im[l], s5_d[l], s5_glu_w[l], s5_glu_b[l])
        y_b = _sb(p, sb_q_gain[l], sb_k_gain[l])
        y_c = _rwkv(p, rw_mu[l], rw_w0[l], rw_w_a[l], rw_w_b[l], rw_a0[l], rw_a_a[l], rw_a_b[l],
                    rw_g_a[l], rw_g_b[l], rw_k_k[l], rw_k_a[l], rw_r_k[l], rw_ln_g[l], rw_ln_b[l])
        y_d = _rglru(p, rg_conv_w[l], rg_conv_b[l], rg_w_x[l], rg_b_x[l], rg_w_a[l], rg_b_a[l],
                     rg_lambda[l])
        x = _outproj(x, (y_a, y_b, y_c, y_d), out_norm[l], w_out, l)
        x = _ffn(x, ffn2_norm[l], ffn2_w1, ffn2_w3, ffn2_w2, l)
    return x.reshape(bn, t, D_MODEL)
```

```python
import functools
import math

import jax
import jax.numpy as jnp
from jax import lax
from jax.experimental import pallas as pl
from jax.experimental.pallas import tpu as pltpu

F32 = jnp.float32
BF16 = jnp.bfloat16

D_MODEL = 2048
GROUP_W = 512
D_FF = 5632
D_IN = 10 * GROUP_W
S5_CH = 16
S5_GROUPS = 32
S5_STATE = 64
S5_LANES = S5_GROUPS * S5_STATE
SB_HEADS = 4
SB_HEAD_DIM = 128
RW_HEAD_DIM = 64
RW_GN_EPS = 64e-5
RG_CONV = 4
RG_C = 8.0
NORM_EPS = 1e-6
LOG2E = 1.4426950408889634

SUBLANES = 8
LANES = 128
MXU_DIM = 256
VMEM_LIMIT = 56 * 1024 * 1024
FFN_VMEM_LIMIT = 62 * 1024 * 1024
ROW_DMA_PRIORITY = 1


def _cparams(*sem):
    return pltpu.CompilerParams(dimension_semantics=sem, vmem_limit_bytes=VMEM_LIMIT)


def _dot(a, b):
    return jnp.dot(a, b, preferred_element_type=F32)


def _dot_nt(a, b):
    return lax.dot_general(a, b, (((1,), (1,)), ((), ())), preferred_element_type=F32)


def _split_dot(x, m_bf16):
    hi = x.astype(BF16)
    lo = (x - hi.astype(F32)).astype(BF16)
    return _dot(hi, m_bf16) + _dot(lo, m_bf16)


def _rms(x, g):
    return x * lax.rsqrt(jnp.mean(x * x, axis=-1, keepdims=True) + NORM_EPS) * g


def _sigmoid(x):
    return 1.0 / (1.0 + jnp.exp(-x))


def _softplus(x):
    return jnp.maximum(x, 0.0) + jnp.log1p(jnp.exp(-jnp.abs(x)))


def _gelu_tanh(x):
    c = math.sqrt(2.0 / math.pi)
    return 0.5 * x * (1.0 + jnp.tanh(c * (x + 0.044715 * (x * x * x))))


def _ffn_kernel(*refs, tm, emit_next):
    if emit_next:
        (x_hbm, g_ref, gn_ref, w1_ref, w3_ref, w2_ref, o_hbm, hn_hbm,
         acc_ref, h_ref, hn_ref, ld_sem, st_sem, hn_sem) = refs
    else:
        x_hbm, g_ref, w1_ref, w3_ref, w2_ref, o_hbm, acc_ref, h_ref, ld_sem, st_sem = refs
    i, j = pl.program_id(0), pl.program_id(1)
    ni, nj = pl.num_programs(0), pl.num_programs(1)
    slot = i % 2
    rows = lambda tile: pl.ds(pl.multiple_of(tile * tm, tm), tm)
    load = lambda tile, s: pltpu.make_async_copy(x_hbm.at[rows(tile), :], acc_ref.at[s], ld_sem.at[s])
    store = lambda tile, s: pltpu.make_async_copy(acc_ref.at[s], o_hbm.at[rows(tile), :], st_sem.at[s])
    store_next = lambda tile: pltpu.make_async_copy(hn_ref, hn_hbm.at[rows(tile), :], hn_sem.at[0])

    @pl.when(jnp.logical_and(i == 0, j == 0))
    def _():
        load(0, 0).start()

    @pl.when(j == 0)
    def _():
        load(i, slot).wait()
        h_ref[...] = _rms(acc_ref[slot], g_ref[...]).astype(BF16)

    @pl.when(j == 1)
    def _():
        @pl.when(i >= 1)
        def _():
            store(i - 1, 1 - slot).wait()

        @pl.when(i + 1 < ni)
        def _():
            load(i + 1, 1 - slot).start(priority=ROW_DMA_PRIORITY)

    h = h_ref[...]
    a = _dot(h, w1_ref[...].astype(BF16))
    b = _dot(h, w3_ref[...].astype(BF16))
    act = (0.5 * (a * _sigmoid(a)) * b).astype(BF16)
    acc_ref[slot] += _dot(act, w2_ref[...].astype(BF16))

    @pl.when(j == nj - 1)
    def _():
        store(i, slot).start(priority=ROW_DMA_PRIORITY)
        if emit_next:
            @pl.when(i >= 1)
            def _():
                store_next(i - 1).wait()

            hn_ref[...] = _rms(acc_ref[slot], gn_ref[...]).astype(BF16)
            store_next(i).start(priority=ROW_DMA_PRIORITY)

        @pl.when(i == ni - 1)
        def _():
            store(i, slot).wait()
            if emit_next:
                store_next(i).wait()


def _ffn(x, g, w1, w3, w2, l, g_next=None, tm=1024, tf=512):
    t = x.shape[0]
    assert D_FF // tf >= 2 and t // tm >= 2
    emit = g_next is not None
    vec = pl.BlockSpec((1, D_MODEL), lambda i, j: (0, 0))
    hbm = pl.BlockSpec(memory_space=pl.ANY)
    out = pl.pallas_call(
        functools.partial(_ffn_kernel, tm=tm, emit_next=emit),
        grid=(t // tm, D_FF // tf),
        in_specs=[hbm, vec] + [vec] * emit + [
            pl.BlockSpec((None, D_MODEL, tf), lambda i, j: (l, 0, j)),
            pl.BlockSpec((None, D_MODEL, tf), lambda i, j: (l, 0, j)),
            pl.BlockSpec((None, tf, D_MODEL), lambda i, j: (l, j, 0)),
        ],
        out_specs=[hbm] + [hbm] * emit,
        out_shape=[jax.ShapeDtypeStruct((t, D_MODEL), F32)] + [jax.ShapeDtypeStruct((t, D_MODEL), BF16)] * emit,
        scratch_shapes=([pltpu.VMEM((2, tm, D_MODEL), F32), pltpu.VMEM((tm, D_MODEL), BF16)]
                        + [pltpu.VMEM((tm, D_MODEL), BF16)] * emit
                        + [pltpu.SemaphoreType.DMA((2,)), pltpu.SemaphoreType.DMA((2,))]
                        + [pltpu.SemaphoreType.DMA((1,))] * emit),
        compiler_params=pltpu.CompilerParams(dimension_semantics=("arbitrary", "arbitrary"),
                                             vmem_limit_bytes=FFN_VMEM_LIMIT),
        name="ffn",
    )(x, g.reshape(1, D_MODEL), *([g_next.reshape(1, D_MODEL)] if emit else []), w1, w3, w2)
    return tuple(out) if emit else out[0]


def _inproj_kernel(h_ref, w_ref, o_ref):
    o_ref[...] = _dot(h_ref[...], w_ref[...].astype(BF16))


def _inproj(h, w, l, tm=2048, tn=512):
    t = h.shape[0]
    return pl.pallas_call(
        _inproj_kernel,
        grid=(t // tm, D_IN // tn),
        in_specs=[
            pl.BlockSpec((tm, D_MODEL), lambda i, j: (i, 0)),
            pl.BlockSpec((None, D_MODEL, tn), lambda i, j: (l, 0, j)),
        ],
        out_specs=pl.BlockSpec((tm, tn), lambda i, j: (i, j)),
        out_shape=jax.ShapeDtypeStruct((t, D_IN), F32),
        compiler_params=_cparams("parallel", "arbitrary"),
        name="inproj",
    )(h, w)


def _outproj_kernel(x_ref, ya_ref, yb_ref, yc_ref, yd_ref, g_ref, w_ref, o_ref):
    acc = x_ref[...]
    for n, y_ref in enumerate((ya_ref, yb_ref, yc_ref, yd_ref)):
        yn = _rms(y_ref[...], g_ref[n:n + 1, :]).astype(BF16)
        acc = acc + _dot(yn, w_ref[n * GROUP_W:(n + 1) * GROUP_W, :].astype(BF16))
    o_ref[...] = acc


def _outproj(x, ys, g, w, l, tm=512):
    t = x.shape[0]
    yspec = pl.BlockSpec((tm, GROUP_W), lambda i: (i, 0))
    return pl.pallas_call(
        _outproj_kernel,
        grid=(t // tm,),
        in_specs=[
            pl.BlockSpec((tm, D_MODEL), lambda i: (i, 0)),
            yspec, yspec, yspec, yspec,
            pl.BlockSpec((4, GROUP_W), lambda i: (0, 0)),
            pl.BlockSpec((None, D_MODEL, D_MODEL), lambda i: (l, 0, 0), pipeline_mode=pl.Buffered(1)),
        ],
        out_specs=pl.BlockSpec((tm, D_MODEL), lambda i: (i, 0)),
        out_shape=jax.ShapeDtypeStruct((t, D_MODEL), F32),
        compiler_params=_cparams("parallel"),
        name="outproj",
    )(x, *ys, g.reshape(4, GROUP_W), w)


def _s5_prep_kernel(lr_ref, li_ref, ldt_ref, br_ref, bi_ref, ab_ref, bb_ref):
    lr, li = lr_ref[...], li_ref[...]
    dt = jnp.exp(ldt_ref[...])
    mag = jnp.exp(lr * dt)
    ab_re, ab_im = mag * jnp.cos(li * dt), mag * jnp.sin(li * dt)
    den = lr * lr + li * li
    g_re = ((ab_re - 1.0) * lr + ab_im * li) / den
    g_im = (ab_im * lr - (ab_re - 1.0) * li) / den
    br, bi = br_ref[...], bi_ref[...]
    ab_ref[0:1, :] = ab_re
    ab_ref[1:2, :] = ab_im
    bb_ref[0] = g_re * br - g_im * bi
    bb_ref[1] = g_re * bi + g_im * br


def _s5_prep(lam_re, lam_im, log_dt, b_re, b_im):
    row = lambda z: z.reshape(1, S5_LANES)
    ldt = jnp.broadcast_to(log_dt[:, None], (S5_GROUPS, S5_STATE))
    tr = lambda b: b.transpose(2, 0, 1).reshape(S5_CH, S5_LANES)
    return pl.pallas_call(
        _s5_prep_kernel,
        out_shape=(jax.ShapeDtypeStruct((2, S5_LANES), F32),
                   jax.ShapeDtypeStruct((2, S5_CH, S5_LANES), F32)),
        name="s5_prep",
    )(row(lam_re), row(lam_im), row(ldt), tr(b_re), tr(b_im))


def _cmul(ar, ai, br, bi):
    return ar * br - ai * bi, ar * bi + ai * br


S5_SPLIT = 4


def _s5_kernel(u_ref, ab_ref, bb_ref, cc_ref, d_ref, gw_ref, gb_ref, o_ref,
               s_ref, tab_ref, car_ref, bmat_ref, cmat_ref, *, tc, cw):
    n = S5_LANES
    nb = n // S5_SPLIT
    cb = GROUP_W // S5_SPLIT

    @pl.when(pl.program_id(0) == 0)
    def _():
        ar = jnp.broadcast_to(ab_ref[0:1, :], (SUBLANES, n))
        ai = jnp.broadcast_to(ab_ref[1:2, :], (SUBLANES, n))
        row = lax.broadcasted_iota(jnp.int32, (SUBLANES, n), 0)
        a2r, a2i = _cmul(ar, ai, ar, ai)
        a4r, a4i = _cmul(a2r, a2i, a2r, a2i)
        for k, (pr, pi, sh) in enumerate(((ar, ai, 1), (a2r, a2i, 2), (a4r, a4i, 4))):
            tab_ref[2 * k] = jnp.where(row >= sh, pr, 0.0)
            tab_ref[2 * k + 1] = jnp.where(row >= sh, pi, 0.0)
        pwr, pwi = ar, ai
        for bit, (qr, qi) in ((1, (ar, ai)), (2, (a2r, a2i)), (4, (a4r, a4i))):
            nr, ni = _cmul(pwr, pwi, qr, qi)
            sel = (row & bit) != 0
            pwr, pwi = jnp.where(sel, nr, pwr), jnp.where(sel, ni, pwi)
        tab_ref[6] = pwr
        tab_ref[7] = pwi
        car_ref[...] = jnp.zeros_like(car_ref)
        lane_grp = lax.broadcasted_iota(jnp.int32, (S5_CH, nb), 1) // S5_STATE
        for c in range(S5_SPLIT):
            st = slice(c * nb, (c + 1) * nb)
            for ref, src, sign in ((bmat_ref, bb_ref, 1.0), (cmat_ref, cc_ref, -1.0)):
                re, im = src[0, :, st], src[1, :, st] * sign
                ref[c] = jnp.concatenate(
                    [jnp.concatenate([jnp.where(lane_grp == g, re, 0.0), jnp.where(lane_grp == g, im, 0.0)], axis=1)
                     for g in range(cb // S5_CH)], axis=0).astype(BF16)

    u = u_ref[...]
    ub = u.astype(BF16)
    for c in range(S5_SPLIT):
        s_ref[:, 2 * c * nb:2 * (c + 1) * nb] = _dot(ub[:, c * cb:(c + 1) * cb], bmat_ref[c])

    cols = [(slice(c * nb + k * cw, c * nb + (k + 1) * cw),
             pl.ds(2 * c * nb + k * cw, cw), pl.ds(2 * c * nb + nb + k * cw, cw))
            for c in range(S5_SPLIT) for k in range(nb // cw)]

    def body(i, carries):
        rows = pl.ds(pl.multiple_of(i * SUBLANES, SUBLANES), SUBLANES)
        out = []
        for (st, cre, cim), (cr, ci) in zip(cols, carries):
            xr, xi = s_ref[rows, cre], s_ref[rows, cim]
            for m, sh in enumerate((1, 2, 4)):
                dr, di = _cmul(tab_ref[2 * m, :, st], tab_ref[2 * m + 1, :, st],
                               pltpu.roll(xr, sh, 0), pltpu.roll(xi, sh, 0))
                xr, xi = xr + dr, xi + di
            dr, di = _cmul(tab_ref[6, :, st], tab_ref[7, :, st],
                           jnp.broadcast_to(cr, xr.shape), jnp.broadcast_to(ci, xi.shape))
            xr, xi = xr + dr, xi + di
            s_ref[rows, cre] = xr
            s_ref[rows, cim] = xi
            out.append((xr[SUBLANES - 1:SUBLANES, :], xi[SUBLANES - 1:SUBLANES, :]))
        return tuple(out)

    carries = lax.fori_loop(0, tc // SUBLANES, body,
                            tuple((car_ref[0:1, st], car_ref[1:2, st]) for st, _, _ in cols))
    for (st, _, _), (cr, ci) in zip(cols, carries):
        car_ref[0:1, st] = cr
        car_ref[1:2, st] = ci

    y = jnp.concatenate([_dot_nt(s_ref[:, 2 * c * nb:2 * (c + 1) * nb].astype(BF16), cmat_ref[c])
                         for c in range(S5_SPLIT)], axis=1) + d_ref[...] * u
    v = _gelu_tanh(y)
    o_ref[...] = v * _sigmoid(_dot(v.astype(BF16), gw_ref[...]) + gb_ref[...])


def _s5(p, ab, bb, c_re, c_im, d, glu_w, glu_b, tc=512, cw=256):
    t = p.shape[0]
    n = S5_LANES
    cc = jnp.stack([z.transpose(1, 0, 2).reshape(S5_CH, n) for z in (c_re, c_im)])
    const = lambda shape: pl.BlockSpec(shape, lambda i: (0,) * len(shape))
    blk = (S5_SPLIT, GROUP_W // S5_SPLIT, 2 * n // S5_SPLIT)
    return pl.pallas_call(
        functools.partial(_s5_kernel, tc=tc, cw=cw),
        grid=(t // tc,),
        in_specs=[
            pl.BlockSpec((tc, GROUP_W), lambda i: (i, 0)),
            const((2, n)), const((2, S5_CH, n)), const((2, S5_CH, n)),
            const((1, GROUP_W)), const((GROUP_W, GROUP_W)), const((1, GROUP_W)),
        ],
        out_specs=pl.BlockSpec((tc, GROUP_W), lambda i: (i, 0)),
        out_shape=jax.ShapeDtypeStruct((t, GROUP_W), F32),
        scratch_shapes=[pltpu.VMEM((tc, 2 * n), F32),
                        pltpu.VMEM((8, SUBLANES, n), F32),
                        pltpu.VMEM((2, n), F32),
                        pltpu.VMEM(blk, BF16), pltpu.VMEM(blk, BF16)],
        compiler_params=_cparams("arbitrary"),
        name="s5",
    )(p, ab, bb, cc, d.reshape(1, GROUP_W), glu_w.astype(BF16), glu_b.reshape(1, GROUP_W))


def _sb_kernel(q_ref, k_ref, v_ref, qg_ref, kg_ref, tri_ref, o_ref, kn_ref, vb_ref, qn_ref, acc_ref,
               z_ref, d_ref, after_ref, w_ref, *, tq, tk, ts):
    qi = pl.program_id(1)
    nd = tq // tk
    nfull = qi * nd

    @pl.when(qi == 0)
    def _():
        kn_ref[...] = _rms(k_ref[...], kg_ref[...]).astype(BF16)
        vb_ref[...] = v_ref[...].astype(BF16)

    qn_ref[...] = (_rms(q_ref[...], qg_ref[...]) * (SB_HEAD_DIM ** -0.5)).astype(BF16)
    tri = tri_ref[...]
    diff = (lax.broadcasted_iota(jnp.int32, (ts, tk), 1) - lax.broadcasted_iota(jnp.int32, (ts, tk), 0))
    subs = tuple(range(0, tq, ts))
    krows = lambda tile: pl.ds(pl.multiple_of(tile * tk, tk), tk)

    def scores(tile):
        return _dot_nt(qn_ref[...], kn_ref[krows(tile), :])

    ntiles = nd + nfull
    hidden = -1e30

    def step(it, par, state, ds, do_q=True, do_n=True, do_w=True, do_p=True):
        cs, rs = state
        if do_q:
            z_next = scores(jnp.maximum(ntiles - 2 - it, 0))
        if do_p:
            pv = _dot(w_ref[par], vb_ref[krows(ntiles + 1 - it), :])
        if do_n:
            dvs, afters, new_rs = [], [], []
            for r0, d in zip(subs, ds):
                if d is False:
                    dvs.append(jnp.full((ts, tk), hidden, F32))
                    afters.append(jnp.zeros((ts, tk), F32))
                    new_rs.append(jnp.zeros((ts, 1), F32))
                    continue
                z = z_ref[par, r0:r0 + ts, :]
                nl = jnp.maximum(z, 0.0) + jnp.log(1.0 + jnp.exp2(jnp.abs(z) * (-LOG2E)))
                dv = z - nl
                if d is not None:
                    nl, dv = jnp.where(diff < d, nl, 0.0), jnp.where(diff < d, dv, hidden)
                dvs.append(dv)
                afters.append(_dot(nl.astype(BF16), tri))
                new_rs.append(jnp.sum(nl, axis=-1, keepdims=True))
            d_ref[par] = jnp.concatenate(dvs, axis=0)
        if do_w:
            ws = [jnp.exp(d_ref[1 - par, r0:r0 + ts, :] - after_ref[1 - par, r0:r0 + ts, :] - c).astype(BF16)
                  for r0, c in zip(subs, cs)]
            w_ref[1 - par] = jnp.concatenate(ws, axis=0)
            cs = tuple(c + r for c, r in zip(cs, rs))
        if do_n:
            after_ref[par] = jnp.concatenate(afters, axis=0)
            rs = tuple(new_rs)
        if do_q:
            z_ref[1 - par] = z_next
        if do_p:
            acc_ref[...] += pv
        return cs, rs

    z_ref[0] = scores(ntiles - 1)
    acc_ref[...] = jnp.zeros_like(acc_ref)
    zeros = tuple(jnp.zeros((ts, 1), F32) for _ in subs)
    state = (zeros, zeros)
    for it in range(nd):
        kd = (nd - 1 - it) * tk
        ds = [False if r0 - kd <= -(ts - 1) else (None if r0 - kd >= tk else r0 - kd) for r0 in subs]
        state = step(it, it % 2, state, ds, do_w=it >= 1, do_p=it >= 2)

    def two_trips(n, st):
        for par in range(2):
            st = step(nd + 2 * n + par, par, st, [None] * len(subs))
        return st

    state = lax.fori_loop(0, nfull // 2, two_trips, state)
    step(ntiles, 0, state, None, do_q=False, do_n=False)
    o_ref[...] = acc_ref[...] + _dot(w_ref[1], vb_ref[krows(0), :])


def _sb(p, q_gain, k_gain, tq=512, tk=256, ts=128):
    t = p.shape[0]
    tri = jnp.tril(jnp.ones((tk, tk), F32), -1).astype(BF16)
    return pl.pallas_call(
        functools.partial(_sb_kernel, tq=tq, tk=tk, ts=ts),
        grid=(SB_HEADS, t // tq),
        in_specs=[
            pl.BlockSpec((tq, SB_HEAD_DIM), lambda h, i: (i, 4 + h)),
            pl.BlockSpec((t, SB_HEAD_DIM), lambda h, i: (0, 8 + h)),
            pl.BlockSpec((t, SB_HEAD_DIM), lambda h, i: (0, 12 + h)),
            pl.BlockSpec((1, SB_HEAD_DIM), lambda h, i: (0, 0)),
            pl.BlockSpec((1, SB_HEAD_DIM), lambda h, i: (0, 0)),
            pl.BlockSpec((tk, tk), lambda h, i: (0, 0)),
        ],
        out_specs=pl.BlockSpec((tq, SB_HEAD_DIM), lambda h, i: (i, h)),
        out_shape=jax.ShapeDtypeStruct((t, GROUP_W), F32),
        scratch_shapes=[pltpu.VMEM((t, SB_HEAD_DIM), BF16), pltpu.VMEM((t, SB_HEAD_DIM), BF16),
                        pltpu.VMEM((tq, SB_HEAD_DIM), BF16), pltpu.VMEM((tq, SB_HEAD_DIM), F32),
                        pltpu.VMEM((2, tq, tk), F32), pltpu.VMEM((2, tq, tk), F32),
                        pltpu.VMEM((2, tq, tk), F32), pltpu.VMEM((2, tq, tk), BF16)],
        compiler_params=_cparams("arbitrary", "arbitrary"),
        name="stickbreak",
    )(p, p, p, q_gain.reshape(1, SB_HEAD_DIM), k_gain.reshape(1, SB_HEAD_DIM), tri)


RW_CHUNK = 64
(_MU_R, _MU_K, _MU_V, _MU_W, _MU_A, _MU_G, _W0, _A0, _KK, _KA, _RK, _LNG, _LNB) = range(13)


_BNN = (((2,), (1,)), ((0,), (0,)))
_BNT = (((2,), (2,)), ((0,), (0,)))


def _bmm(a, b, dims=_BNN):
    return lax.dot_general(a.astype(BF16), b.astype(BF16), dims, preferred_element_type=F32)


def _split_dot_left(m_bf16, x):
    hi = x.astype(BF16)
    lo = (x - hi.astype(F32)).astype(BF16)
    return _dot(m_bf16, hi) + _dot(m_bf16, lo)


def _rwkv_kernel(r_ref, k_ref, v_ref, m_ref, vec_ref, wa_ref, wb_ref, aa_ref, ab_ref, ga_ref, gb_ref,
                 ones_ref, tril_ref, blk_ref, o_ref, zs_ref, st_ref, *, tc):
    first = pl.program_id(0) == 0
    w = GROUP_W
    vec = lambda i: vec_ref[i:i + 1, :]

    @pl.when(first)
    def _():
        zs_ref[...] = jnp.zeros_like(zs_ref)
        st_ref[...] = jnp.zeros_like(st_ref)

    cur, prev = [], []
    for n, ref in enumerate((r_ref, k_ref, v_ref, m_ref)):
        zs_ref[n, SUBLANES - 1:SUBLANES, :] = zs_ref[n, SUBLANES + tc - 1:SUBLANES + tc, :]
        z = ref[...]
        zs_ref[n, SUBLANES:SUBLANES + tc, :] = z
        cur.append(z)
        prev.append(zs_ref[n, SUBLANES - 1:SUBLANES - 1 + tc, :])
    lerp = lambda n, mu: cur[n] + (prev[n] - cur[n]) * vec(mu)
    r, k, v = lerp(0, _MU_R), lerp(1, _MU_K), lerp(2, _MU_V)
    xw, xa, xg = lerp(3, _MU_W), lerp(3, _MU_A), lerp(3, _MU_G)

    lora = lambda x, a_ref, b_ref, f: _dot(f(_dot(x.astype(BF16), a_ref[...])).astype(BF16), b_ref[...])
    w_log = -_softplus(-(vec(_W0) + lora(xw, wa_ref, wb_ref, jnp.tanh))) - 0.5
    lw = -jnp.exp(w_log)
    a = _sigmoid(vec(_A0) + lora(xa, aa_ref, ab_ref, lambda z: z))
    g = lora(xg, ga_ref, gb_ref, _sigmoid)
    ones = ones_ref[...]

    def head_sums(x):
        return jnp.concatenate([_split_dot(x[:, c:c + MXU_DIM], ones) for c in range(0, w, MXU_DIM)], axis=1)

    kk = k * vec(_KK)
    kk = kk * lax.rsqrt(head_sums(kk * kk) + 1e-12)
    k = k * (1.0 + (a - 1.0) * vec(_KA))
    bonus = head_sums(r * k * vec(_RK)) * v
    a_in, b_in = -kk, kk * a

    L = RW_CHUNK
    nch, nhp = tc // L, w // LANES
    cum = _split_dot_left(tril_ref[...], lw)
    cum_l = _split_dot_left(blk_ref[...], lw)
    g_inv, g_end = jnp.exp(-cum), jnp.exp(cum_l - cum)
    head0 = lax.broadcasted_iota(jnp.int32, (L, LANES), 1) < RW_HEAD_DIM

    def stacked(x, split_heads=True):
        out = []
        for c in range(nch):
            for hp in range(nhp):
                z = x[c * L:(c + 1) * L, hp * LANES:(hp + 1) * LANES]
                out.append(jnp.concatenate([jnp.where(head0, z, 0.0), jnp.where(head0, 0.0, z)], axis=0)
                           if split_heads else jnp.concatenate([z, z], axis=0))
        return jnp.stack(out)

    xa_, xr_ = stacked(a_in * jnp.exp(cum - lw)).astype(BF16), stacked(r * jnp.exp(cum)).astype(BF16)
    xb_, xk_, xv_ = stacked(b_in * g_inv).astype(BF16), stacked(k * g_inv).astype(BF16), stacked(v).astype(BF16)
    xbl_t = jnp.swapaxes(stacked(b_in * g_end), 1, 2).astype(BF16)
    xkl_t = jnp.swapaxes(stacked(k * g_end), 1, 2).astype(BF16)
    g_col = jnp.swapaxes(stacked(jnp.exp(cum_l), split_heads=False), 1, 2)

    ri = lax.broadcasted_iota(jnp.int32, (2 * L, 2 * L), 0)
    ci = lax.broadcasted_iota(jnp.int32, (2 * L, 2 * L), 1)
    strict, incl = ri > ci, ri >= ci
    eye = (ri == ci).astype(F32)
    pm = _bmm(jnp.concatenate([xa_, xr_], axis=1), jnp.concatenate([xb_, xk_], axis=1), _BNT)
    n_ab = jnp.where(strict, pm[:, :2 * L, :2 * L], 0.0)
    a_ak = jnp.where(strict, pm[:, :2 * L, 2 * L:], 0.0)
    a_rb = jnp.where(incl, pm[:, 2 * L:, :2 * L], 0.0).astype(BF16)
    a_rk = jnp.where(incl, pm[:, 2 * L:, 2 * L:], 0.0)
    tinv, npow = eye + n_ab, n_ab
    for _ in range(int(math.log2(L)) - 1):
        npow = _bmm(npow, npow)
        tinv = tinv + _bmm(npow, tinv)
    wu = _bmm(tinv, jnp.concatenate([xa_, _bmm(a_ak, xv_).astype(BF16)], axis=2))
    wx = jnp.concatenate([wu[:, :, :LANES].astype(BF16), xr_], axis=1)
    u0 = wu[:, :, LANES:]
    y0 = _bmm(a_rk, xv_)
    s0 = _bmm(xkl_t, xv_)

    st = st_ref[...]
    ys = []
    for c in range(nch):
        sl = slice(c * nhp, (c + 1) * nhp)
        gs = _bmm(wx[sl], st)
        u = (gs[:, :2 * L] + u0[sl]).astype(BF16)
        yst = gs[:, 2 * L:] + _bmm(a_rb[sl], u) + y0[sl]
        st = g_col[sl] * st + _bmm(xbl_t[sl], u) + s0[sl]
        ys.append(jnp.concatenate([yst[hp, :L] + yst[hp, L:] for hp in range(nhp)], axis=1))
    st_ref[...] = st

    y = jnp.concatenate(ys, axis=0)
    inv_n = 1.0 / RW_HEAD_DIM
    mean = head_sums(y) * inv_n
    yc = y - mean
    var = head_sums(yc * yc) * inv_n
    y = yc * lax.rsqrt(var + RW_GN_EPS) * vec(_LNG) + vec(_LNB)
    o_ref[...] = (y + bonus) * g


def _rwkv(p, mu, w0, w_a, w_b, a0, a_a, a_b, g_a, g_b, k_k, k_a, r_k, ln_g, ln_b, tc=256):
    t = p.shape[0]
    w = GROUP_W
    vecs = jnp.concatenate([mu, jnp.stack([w0, a0, k_k, k_a, r_k.reshape(w), ln_g, ln_b]),
                            jnp.zeros((3, w), F32)], axis=0)
    head = jnp.arange(MXU_DIM) // RW_HEAD_DIM
    ones = (head[:, None] == head[None, :]).astype(BF16)
    chunk = jnp.arange(tc) // RW_CHUNK
    blk = chunk[:, None] == chunk[None, :]
    tril = jnp.logical_and(blk, jnp.arange(tc)[:, None] >= jnp.arange(tc)[None, :])
    const = lambda shape: pl.BlockSpec(shape, lambda i: (0,) * len(shape))
    col = lambda c: pl.BlockSpec((tc, w), lambda i: (i, c))
    bf = lambda z: z.astype(BF16)
    return pl.pallas_call(
        functools.partial(_rwkv_kernel, tc=tc),
        grid=(t // tc,),
        in_specs=[col(4), col(5), col(6), col(7), const((16, w)),
                  const(w_a.shape), const(w_b.shape), const(a_a.shape), const(a_b.shape),
                  const(g_a.shape), const(g_b.shape), const((MXU_DIM, MXU_DIM)), const((tc, tc)), const((tc, tc))],
        out_specs=pl.BlockSpec((tc, w), lambda i: (i, 0)),
        out_shape=jax.ShapeDtypeStruct((t, w), F32),
        scratch_shapes=[pltpu.VMEM((4, tc + SUBLANES, w), F32), pltpu.VMEM((w // LANES, LANES, LANES), F32)],
        compiler_params=_cparams("arbitrary"),
        name="rwkv7",
    )(p, p, p, p, vecs, bf(w_a), bf(w_b), bf(a_a), bf(a_b), bf(g_a), bf(g_b), ones, bf(tril), bf(blk))


def _rglru_kernel(gate_ref, x_ref, cw_ref, vec_ref, wx_ref, wa_ref, o_ref,
                  xs_ref, a_s, b_s, car_ref, *, tc):
    first = pl.program_id(0) == 0
    halo = SUBLANES

    @pl.when(first)
    def _():
        xs_ref[...] = jnp.zeros_like(xs_ref)
        car_ref[...] = jnp.zeros_like(car_ref)

    xs_ref[0:halo, :] = xs_ref[tc:tc + halo, :]
    xs_ref[halo:halo + tc, :] = x_ref[...]
    xc = vec_ref[0:1, :]
    for j in range(RG_CONV):
        off = halo - (RG_CONV - 1) + j
        xc = xc + xs_ref[off:off + tc, :] * cw_ref[j:j + 1, :]
    xcb = xc.astype(BF16)
    gate_x = _sigmoid(_dot(xcb, wx_ref[...]) + vec_ref[1:2, :])
    gate_a = _sigmoid(_dot(xcb, wa_ref[...]) + vec_ref[2:3, :])
    lam = vec_ref[3:4, :]
    log_a = RG_C * gate_a * (-_softplus(-lam))
    a = jnp.exp(log_a)
    mult = jnp.sqrt(1.0 - a * a)
    t_idx = lax.broadcasted_iota(jnp.int32, (tc, GROUP_W), 0)
    mult = jnp.where(jnp.logical_and(first, t_idx == 0), 1.0, mult)
    a_s[...] = a
    b_s[...] = xc * gate_x * mult

    row = lax.broadcasted_iota(jnp.int32, (SUBLANES, GROUP_W), 0)

    def body(i, carry):
        rows = pl.ds(pl.multiple_of(i * SUBLANES, SUBLANES), SUBLANES)
        ab, hb = a_s[rows, :], b_s[rows, :]
        for sh in (1, 2, 4):
            ash = jnp.where(row >= sh, pltpu.roll(ab, sh, 0), 1.0)
            hsh = jnp.where(row >= sh, pltpu.roll(hb, sh, 0), 0.0)
            hb = hb + ab * hsh
            ab = ab * ash
        hb = hb + ab * jnp.broadcast_to(carry, hb.shape)
        b_s[rows, :] = hb
        return hb[SUBLANES - 1:SUBLANES, :]

    car_ref[...] = lax.fori_loop(0, tc // SUBLANES, body, car_ref[...], unroll=2)
    o_ref[...] = _gelu_tanh(gate_ref[...]) * b_s[...]


def _rglru(p, conv_w, conv_b, w_x, b_x, w_a, b_a, lam, tc=512):
    t = p.shape[0]
    w = GROUP_W
    blk = jnp.arange(w) // (w // w_x.shape[0])
    same = blk[:, None] == blk[None, :]
    bdiag = lambda m: jnp.where(same, jnp.tile(m.reshape(w, -1), (1, m.shape[0])), 0.0).astype(BF16)
    vecs = jnp.concatenate([jnp.stack([conv_b, b_x, b_a, lam]), jnp.zeros((4, w), F32)], axis=0)
    const = lambda shape: pl.BlockSpec(shape, lambda i: (0,) * len(shape))
    return pl.pallas_call(
        functools.partial(_rglru_kernel, tc=tc),
        grid=(t // tc,),
        in_specs=[pl.BlockSpec((tc, w), lambda i: (i, 8)), pl.BlockSpec((tc, w), lambda i: (i, 9)),
                  const((RG_CONV, w)), const((8, w)), const((w, w)), const((w, w))],
        out_specs=pl.BlockSpec((tc, w), lambda i: (i, 0)),
        out_shape=jax.ShapeDtypeStruct((t, w), F32),
        scratch_shapes=[pltpu.VMEM((tc + SUBLANES, w), F32), pltpu.VMEM((tc, w), F32),
                        pltpu.VMEM((tc, w), F32), pltpu.VMEM((1, w), F32)],
        compiler_params=_cparams("arbitrary"),
        name="rglru",
    )(p, p, conv_w, vecs, bdiag(w_x), bdiag(w_a))


def kernel(x, ffn1_norm, ffn1_w1, ffn1_w3, ffn1_w2, mix_norm, w_in, s5_lambda_re, s5_lambda_im, s5_log_dt, s5_b_re, s5_b_im, s5_c_re, s5_c_im, s5_d, s5_glu_w, s5_glu_b, sb_q_gain, sb_k_gain, rw_mu, rw_w0, rw_w_a, rw_w_b, rw_a0, rw_a_a, rw_a_b, rw_g_a, rw_g_b, rw_k_k, rw_k_a, rw_r_k, rw_ln_g, rw_ln_b, rg_conv_w, rg_conv_b, rg_w_x, rg_b_x, rg_w_a, rg_b_a, rg_lambda, out_norm, w_out, ffn2_norm, ffn2_w1, ffn2_w3, ffn2_w2):
    bn, t, _ = x.shape
    assert bn == 1
    x = x.reshape(t, D_MODEL)
    for l in range(ffn1_norm.shape[0]):
        x, h = _ffn(x, ffn1_norm[l], ffn1_w1, ffn1_w3, ffn1_w2, l, g_next=mix_norm[l])
        p = _inproj(h, w_in, l)
        ab, bb = _s5_prep(s5_lambda_re[l], s5_lambda_im[l], s5_log_dt[l], s5_b_re[l], s5_b_im[l])
        y_a = _s5(p, ab, bb, s5_c_re[l], s5_c_im[l], s5_d[l], s5_glu_w[l], s5_glu_b[l])
        y_b = _sb(p, sb_q_gain[l], sb_k_gain[l])
        y_c = _rwkv(p, rw_mu[l], rw_w0[l], rw_w_a[l], rw_w_b[l], rw_a0[l], rw_a_a[l], rw_a_b[l],
                    rw_g_a[l], rw_g_b[l], rw_k_k[l], rw_k_a[l], rw_r_k[l], rw_ln_g[l], rw_ln_b[l])
        y_d = _rglru(p, rg_conv_w[l], rg_conv_b[l], rg_w_x[l], rg_b_x[l], rg_w_a[l], rg_b_a[l],
                     rg_lambda[l])
        x = _outproj(x, (y_a, y_b, y_c, y_d), out_norm[l], w_out, l)
        x = _ffn(x, ffn2_norm[l], ffn2_w1, ffn2_w3, ffn2_w2, l)
    return x.reshape(bn, t, D_MODEL)
```
